```python
import math
import jax
import jax.numpy as jnp
from jax import lax
import numpy as np

D_MODEL = 2048
BATCH = 1
SEQ = 8192
DEPTH = 2
DEC_BATCH = 128
DEC_SEQ = 4
PAST_LEN = 2048
PAGE_SIZE = 128

BR_WIDTH = 3 * D_MODEL // 8
N_BRANCH = 4
A_HEAD_DIM = 64
A_HEADS = BR_WIDTH // A_HEAD_DIM
A_PATTERNS = ((128, 1), (512, 4), (2048, 16))
A_MAX_WINDOW = max(w for w, _ in A_PATTERNS)
A_BLOCK = 128
B_HEAD_DIM = 64
B_HEADS = BR_WIDTH // B_HEAD_DIM
B_GROUPS = 4
B_STATE = 128
B_CONV = 4
B_CHUNK = 128
B_CONV_DIM = BR_WIDTH + 2 * B_GROUPS * B_STATE
C_GROUP = 16
C_NGROUPS = BR_WIDTH // C_GROUP
C_STATE = 64
D_HEAD_DIM = 64
D_HEADS = BR_WIDTH // D_HEAD_DIM
D_LORA_W = 64
D_LORA_A = 64
D_LORA_G = 128
D_SPLITS = (BR_WIDTH, D_LORA_W, BR_WIDTH, BR_WIDTH, D_LORA_A, D_LORA_G)
D_COLS = sum(D_SPLITS)
IN_SPLITS = (BR_WIDTH, BR_WIDTH, BR_WIDTH,
             BR_WIDTH, B_CONV_DIM, B_HEADS,
             BR_WIDTH,
             D_COLS,
             N_BRANCH * D_MODEL)
IN_COLS = sum(IN_SPLITS)
D_FF = 4 * D_MODEL
NORM_EPS = 1e-6
RWKV_GN_EPS = 64e-5

kernel_name = 'hybrid_parallel_gated_decoder_step'


def _rmsnorm(x, g, eps=NORM_EPS):
    x32 = x.astype(jnp.float32)
    y = x32 * lax.rsqrt(jnp.mean(x32 * x32, axis=-1, keepdims=True) + eps)
    return y * g.astype(jnp.float32)


def _split(a, sizes):
    return jnp.split(a, np.cumsum(sizes)[:-1].tolist(), axis=-1)


def _dilated_prompt(q, k, v, window, dil):
    b, s, h, e = q.shape
    wk = window // dil
    n = s // dil
    nb = -(-n // A_BLOCK)
    npad = nb * A_BLOCK

    def sub(a, front):
        a = a.reshape(b, n, dil, h, e).transpose(0, 2, 1, 3, 4)
        return jnp.pad(a, ((0, 0), (0, 0), (front, npad - n), (0, 0), (0, 0)))

    qb = sub(q, 0).reshape(b, dil, nb, A_BLOCK, h, e)
    idx = jnp.arange(nb)[:, None] * A_BLOCK + jnp.arange(A_BLOCK + wk)[None, :]
    kb = sub(k, wk)[:, :, idx]
    vb = sub(v, wk)[:, :, idx]
    sc = jnp.einsum('bdnqhe,bdnkhe->bdnhqk', qb, kb)
    qi = jnp.arange(A_BLOCK)[:, None]
    kj = jnp.arange(A_BLOCK + wk)[None, :]
    band = (kj >= qi) & (kj <= qi + wk)
    live = (idx >= wk)[:, None, :]
    mask = band[None] & live
    sc = jnp.where(mask[None, None, :, None], sc, -jnp.inf)
    mx = jnp.max(sc, axis=-1)
    p = jnp.exp(sc - mx[..., None])
    den = jnp.sum(p, axis=-1)
    o = jnp.einsum('bdnhqk,bdnkhe->bdnqhe', p, vb) / jnp.swapaxes(den, -1, -2)[..., None]

    def unsub(a):
        a = a.reshape((b, dil, npad) + a.shape[4:])[:, :, :n]
        return jnp.moveaxis(a, 1, 2).reshape((b, s) + a.shape[3:])

    return unsub(o), unsub(jnp.swapaxes(mx, -1, -2)), unsub(jnp.swapaxes(den, -1, -2))


def _dilated_sample(q, kc, vc, n_past, window, dil):
    t = q.shape[1]
    wk = window // dil
    idx = n_past + jnp.arange(t)[:, None] - dil * jnp.arange(wk + 1)[None, :]
    valid = idx >= 0
    idx = jnp.maximum(idx, 0)
    sc = jnp.einsum('bthe,btkhe->bthk', q, kc[:, idx])
    sc = jnp.where(valid[None, :, None, :], sc, -jnp.inf)
    mx = jnp.max(sc, axis=-1)
    p = jnp.exp(sc - mx[..., None])
    den = jnp.sum(p, axis=-1)
    o = jnp.einsum('bthk,btkhe->bthe', p, vc[:, idx]) / den[..., None]
    return o, mx, den


def _mixer_a(qa, ka, va, q_gain, k_gain, kv_buf):
    b, l, _ = qa.shape
    shp = (b, l, A_HEADS, A_HEAD_DIM)
    q = _rmsnorm(qa.reshape(shp), q_gain) * (A_HEAD_DIM ** -0.5)
    k = _rmsnorm(ka.reshape(shp), k_gain)
    v = va.reshape(shp).astype(jnp.float32)
    if kv_buf is None:
        outs = [_dilated_prompt(q, k, v, w, d) for w, d in A_PATTERNS]
        kv_new = jnp.stack([k, v], axis=2)[:, l - min(A_MAX_WINDOW, l):]
    else:
        n_past = kv_buf.shape[1]
        kc = jnp.concatenate([kv_buf[:, :, 0].astype(jnp.float32), k], axis=1)
        vc = jnp.concatenate([kv_buf[:, :, 1].astype(jnp.float32), v], axis=1)
        outs = [_dilated_sample(q, kc, vc, n_past, w, d) for w, d in A_PATTERNS]
        kv_new = jnp.stack([k, v], axis=2)
    o = jnp.stack([r[0] for r in outs])
    mx = jnp.stack([r[1] for r in outs])
    den = jnp.stack([r[2] for r in outs])
    wgt = den * jnp.exp(mx - jnp.max(mx, axis=0, keepdims=True))
    o = jnp.sum(wgt[..., None] * o, axis=0) / jnp.sum(wgt, axis=0)[..., None]
    return o.reshape(b, l, BR_WIDTH), kv_new


def _causal_dwconv(x, prev, w, bias):
    xp = jnp.concatenate([prev.astype(jnp.float32), x], axis=1)
    y = lax.conv_general_dilated(xp, w.astype(jnp.float32)[:, None, :], (1,), 'VALID',
                                 dimension_numbers=('NWC', 'WIO', 'NWC'),
                                 feature_group_count=x.shape[-1])
    return y + bias.astype(jnp.float32), xp[:, xp.shape[1] - (B_CONV - 1):]


def _segsum_exp(a):
    cs = jnp.cumsum(a, axis=-1)
    diff = cs[..., :, None] - cs[..., None, :]
    t = a.shape[-1]
    tri = jnp.tril(jnp.ones((t, t), dtype=bool))
    return jnp.exp(jnp.where(tri, diff, -jnp.inf))


def _ssd(x, dt, a, bm, cm, h0):
    b, l, h, p = x.shape
    g, n = bm.shape[2], bm.shape[3]
    r = h // g
    q = min(B_CHUNK, l)
    c = -(-l // q)
    pad = c * q - l

    def padl(z):
        return jnp.pad(z, ((0, 0), (0, pad)) + ((0, 0),) * (z.ndim - 2))

    x, dt, bm, cm = padl(x), padl(dt), padl(bm), padl(cm)
    xdt = (x * dt[..., None]).reshape(b, c, q, g, r, p)
    adt = (dt * a).reshape(b, c, q, g, r).transpose(0, 1, 3, 4, 2)
    bc = bm.reshape(b, c, q, g, n)
    cc = cm.reshape(b, c, q, g, n)
    a_cum = jnp.cumsum(adt, axis=-1)
    y_diag = jnp.einsum('bcgqs,bcgrqs,bcsgrp->bcqgrp',
                        jnp.einsum('bcqgn,bcsgn->bcgqs', cc, bc), _segsum_exp(adt), xdt)
    decay_to_end = jnp.exp(a_cum[..., -1:] - a_cum)
    chunk_states = jnp.einsum('bcsgn,bcgrs,bcsgrp->bcgrpn', bc, decay_to_end, xdt)
    states = jnp.concatenate([h0.reshape(b, 1, g, r, p, n), chunk_states], axis=1)
    a_last = jnp.pad(jnp.moveaxis(a_cum[..., -1], 1, -1), ((0, 0), (0, 0), (0, 0), (1, 0)))
    states = jnp.einsum('bgrzc,bcgrpn->bzgrpn', _segsum_exp(a_last), states)
    y_off = jnp.einsum('bcqgn,bcgrpn,bcgrq->bcqgrp', cc, states[:, :-1], jnp.exp(a_cum))
    y = (y_diag + y_off).reshape(b, c * q, h, p)[:, :l]
    return y, states[:, -1].reshape(b, h, p, n)


def _mixer_b(z, xbc, dt_raw, conv0, ssm0, conv_w, conv_b, dt_bias, a_log, d_skip, norm_w):
    b, l, _ = z.shape
    xbc, conv_new = _causal_dwconv(xbc.astype(jnp.float32), conv0, conv_w, conv_b)
    xbc = jax.nn.silu(xbc)
    xs, bm, cm = _split(xbc, (BR_WIDTH, B_GROUPS * B_STATE, B_GROUPS * B_STATE))
    xs = xs.reshape(b, l, B_HEADS, B_HEAD_DIM)
    dt = jax.nn.softplus(dt_raw.astype(jnp.float32) + dt_bias.astype(jnp.float32))
    a = -jnp.exp(a_log.astype(jnp.float32))
    y, ssm_new = _ssd(xs, dt, a, bm.reshape(b, l, B_GROUPS, B_STATE),
                      cm.reshape(b, l, B_GROUPS, B_STATE), ssm0.astype(jnp.float32))
    y = y + xs * d_skip.astype(jnp.float32)[:, None]
    y = (y.reshape(b, l, BR_WIDTH) * jax.nn.silu(z.astype(jnp.float32)))
    y = y.reshape(b, l, B_GROUPS, BR_WIDTH // B_GROUPS)
    y = y * lax.rsqrt(jnp.mean(y * y, axis=-1, keepdims=True) + NORM_EPS)
    return y.reshape(b, l, BR_WIDTH) * norm_w.astype(jnp.float32), conv_new, ssm_new


def _complex_affine_combine(e1, e2):
    a1r, a1i, b1r, b1i = e1
    a2r, a2i, b2r, b2i = e2
    return (a2r * a1r - a2i * a1i, a2r * a1i + a2i * a1r,
            a2r * b1r - a2i * b1i + b2r, a2r * b1i + a2i * b1r + b2i)


def _mixer_c(u, s0, a_re, a_im, log_step, b_re, b_im, c_re, c_im, d_skip, w_glu, b_glu):
    f32 = jnp.float32
    b, l, _ = u.shape
    u = u.astype(f32)
    a_re, a_im = a_re.astype(f32), a_im.astype(f32)
    step = jnp.exp(log_step.astype(f32))[:, None]
    mag = jnp.exp(a_re * step)
    lb_re, lb_im = mag * jnp.cos(a_im * step), mag * jnp.sin(a_im * step)
    den = a_re * a_re + a_im * a_im
    f_re = ((lb_re - 1.0) * a_re + lb_im * a_im) / den
    f_im = (lb_im * a_re - (lb_re - 1.0) * a_im) / den
    bb_re = f_re[..., None] * b_re - f_im[..., None] * b_im
    bb_im = f_re[..., None] * b_im + f_im[..., None] * b_re
    ug = u.reshape(b, l, C_NGROUPS, C_GROUP)
    bu_re = jnp.einsum('gpc,blgc->blgp', bb_re, ug)
    bu_im = jnp.einsum('gpc,blgc->blgp', bb_im, ug)
    h_re0, h_im0 = s0[..., 0].astype(f32), s0[..., 1].astype(f32)
    bu_re = bu_re.at[:, 0].add(lb_re * h_re0 - lb_im * h_im0)
    bu_im = bu_im.at[:, 0].add(lb_re * h_im0 + lb_im * h_re0)
    shape = bu_re.shape
    _, _, h_re, h_im = lax.associative_scan(
        _complex_affine_combine,
        (jnp.broadcast_to(lb_re, shape), jnp.broadcast_to(lb_im, shape), bu_re, bu_im), axis=1)
    y = jnp.einsum('gcp,blgp->blgc', c_re, h_re) - jnp.einsum('gcp,blgp->blgc', c_im, h_im)
    y = jax.nn.gelu(y.reshape(b, l, BR_WIDTH) + d_skip * u)
    out = y * jax.nn.sigmoid(y @ w_glu + b_glu)
    return out, jnp.stack([h_re[:, -1], h_im[:, -1]], axis=-1)


def _mixer_d(pd, shift0, wkv0, mu, w0, w_lora, a0, a_lora, g_lora, k_k, k_a, r_k, gn_w, gn_b):
    f32 = jnp.float32
    b, l, _ = pd.shape
    pd = pd.astype(f32)
    prev = jnp.concatenate([shift0[:, None].astype(f32), pd[:, :-1]], axis=1)
    xm = pd + (prev - pd) * mu
    r, wl, k, v, al, gl = _split(xm, D_SPLITS)
    decay = jnp.exp(-jnp.exp(-jax.nn.softplus(-(w0 + jnp.tanh(wl) @ w_lora)) - 0.5))
    a = jax.nn.sigmoid(a0 + al @ a_lora)
    g = jax.nn.sigmoid(gl) @ g_lora
    hs = (b, l, D_HEADS, D_HEAD_DIM)
    kk = (k * k_k).reshape(hs)
    kk = kk / jnp.maximum(jnp.linalg.norm(kk, axis=-1, keepdims=True), 1e-12)
    k = k * (1.0 + (a - 1.0) * k_a)
    r, decay, k, v, a = [t.reshape(hs) for t in (r, decay, k, v, a)]

    def step(s, inp):
        r_t, w_t, k_t, v_t, kk_t, a_t = inp
        s = (s * w_t[:, :, None, :]
             - jnp.einsum('bhij,bhj->bhi', s, kk_t)[..., None] * (kk_t * a_t)[:, :, None, :]
             + v_t[..., None] * k_t[:, :, None, :])
        return s, jnp.einsum('bhij,bhj->bhi', s, r_t)

    seq = tuple(jnp.moveaxis(t, 1, 0) for t in (r, decay, k, v, kk, a))
    wkv_new, y = lax.scan(step, wkv0.astype(f32), seq)
    y = jnp.moveaxis(y, 0, 1)
    mean = jnp.mean(y, axis=-1, keepdims=True)
    var = jnp.mean(jnp.square(y - mean), axis=-1, keepdims=True)
    y = ((y - mean) * lax.rsqrt(var + RWKV_GN_EPS) * gn_w.reshape(D_HEADS, D_HEAD_DIM)
         + gn_b.reshape(D_HEADS, D_HEAD_DIM))
    y = y + jnp.sum(r * k * r_k, axis=-1, keepdims=True) * v
    return y.reshape(b, l, BR_WIDTH) * g, pd[:, -1], wkv_new


def _layer(x, kv_buf, conv0, ssm0, s50, shift0, wkv0, P, i):
    b, l, _ = x.shape
    h = _rmsnorm(x, P['norm1'][i])
    proj = h @ P['w_in'][i]
    qa, ka, va, zb, xbc, dtb, uc, pd, gl = _split(proj, IN_SPLITS)
    oa, kv_new = _mixer_a(qa, ka, va, P['a_q_gain'][i], P['a_k_gain'][i], kv_buf)
    ob, conv_new, ssm_new = _mixer_b(zb, xbc, dtb, conv0, ssm0, P['b_conv_w'][i], P['b_conv_b'][i],
                                     P['b_dt_bias'][i], P['b_a_log'][i], P['b_d'][i], P['b_norm'][i])
    oc, s5_new = _mixer_c(uc, s50, P['c_a_re'][i], P['c_a_im'][i], P['c_log_step'][i],
                          P['c_b_re'][i], P['c_b_im'][i], P['c_c_re'][i], P['c_c_im'][i],
                          P['c_d'][i], P['c_w_glu'][i], P['c_b_glu'][i])
    od, shift_new, wkv_new = _mixer_d(pd, shift0, wkv0, P['d_mu'][i], P['d_w0'][i], P['d_w_lora'][i],
                                      P['d_a0'][i], P['d_a_lora'][i], P['d_g_lora'][i], P['d_k_k'][i],
                                      P['d_k_a'][i], P['d_r_k'][i], P['d_gn_w'][i], P['d_gn_b'][i])
    gates = jax.nn.sigmoid(gl.astype(jnp.float32)).reshape(b, l, N_BRANCH, D_MODEL)
    merged = gates[:, :, 0] * (oa @ P['w_branch'][i, 0])
    for n, o in ((1, ob), (2, oc), (3, od)):
        merged = merged + gates[:, :, n] * (o @ P['w_branch'][i, n])
    x = x + merged @ P['w_out'][i]
    h2 = _rmsnorm(x, P['norm2'][i])
    x = x + jnp.square(jax.nn.relu(h2 @ P['w_ff1'][i])) @ P['w_ff2'][i]
    return x, (kv_new, conv_new, ssm_new, s5_new, shift_new, wkv_new)


def setup_inputs(seed: int = 0) -> dict:
    key = jax.random.key(seed)
    keys = iter(jax.random.split(key, 64))
    f32 = jnp.float32

    def nrm(shape, scale=1.0, mean=0.0):
        return mean + scale * jax.random.normal(next(keys), shape, f32)

    def uni(shape, lo, hi):
        return jax.random.uniform(next(keys), shape, f32, lo, hi)

    la = min(A_MAX_WINDOW, PAST_LEN)
    dt0 = jnp.exp(uni((DEPTH, B_HEADS), math.log(1e-3), math.log(1e-1)))
    return {
        'x_prompt': nrm((BATCH, SEQ, D_MODEL)),
        'x_sample': nrm((DEC_BATCH, DEC_SEQ, D_MODEL)),
        'cache_kv_a': nrm((DEPTH, DEC_BATCH, la, 2, A_HEADS, A_HEAD_DIM)),
        'state_conv': nrm((DEPTH, DEC_BATCH, B_CONV - 1, B_CONV_DIM)),
        'state_ssm': nrm((DEPTH, DEC_BATCH, B_HEADS, B_HEAD_DIM, B_STATE), 0.1),
        'state_s5': nrm((DEPTH, DEC_BATCH, C_NGROUPS, C_STATE, 2), 0.1),
        'state_shift': nrm((DEPTH, DEC_BATCH, D_COLS)),
        'state_wkv': nrm((DEPTH, DEC_BATCH, D_HEADS, D_HEAD_DIM, D_HEAD_DIM), 0.1),
        'norm1': nrm((DEPTH, D_MODEL), 0.02, 1.0),
        'w_in': nrm((DEPTH, D_MODEL, IN_COLS), D_MODEL ** -0.5),
        'a_q_gain': nrm((DEPTH, A_HEAD_DIM), 0.02, 1.0),
        'a_k_gain': nrm((DEPTH, A_HEAD_DIM), 0.02, 1.0),
        'b_conv_w': nrm((DEPTH, B_CONV, B_CONV_DIM), B_CONV ** -0.5),
        'b_conv_b': nrm((DEPTH, B_CONV_DIM), 0.02),
        'b_dt_bias': dt0 + jnp.log(-jnp.expm1(-dt0)),
        'b_a_log': jnp.log(uni((DEPTH, B_HEADS), 1.0, 16.0)),
        'b_d': nrm((DEPTH, B_HEADS), 0.1, 1.0),
        'b_norm': nrm((DEPTH, BR_WIDTH), 0.02, 1.0),
        'c_a_re': nrm((DEPTH, C_NGROUPS, C_STATE), 0.01, -0.5),
        'c_a_im': math.pi * jnp.arange(C_STATE, dtype=f32) + nrm((DEPTH, C_NGROUPS, C_STATE), 0.01),
        'c_log_step': uni((DEPTH, C_NGROUPS), math.log(1e-3), math.log(1e-1)),
        'c_b_re': nrm((DEPTH, C_NGROUPS, C_STATE, C_GROUP), (2 * C_GROUP) ** -0.5),
        'c_b_im': nrm((DEPTH, C_NGROUPS, C_STATE, C_GROUP), (2 * C_GROUP) ** -0.5),
        'c_c_re': nrm((DEPTH, C_NGROUPS, C_GROUP, C_STATE), (2 * C_STATE) ** -0.5),
        'c_c_im': nrm((DEPTH, C_NGROUPS, C_GROUP, C_STATE), (2 * C_STATE) ** -0.5),
        'c_d': nrm((DEPTH, BR_WIDTH)),
        'c_w_glu': nrm((DEPTH, BR_WIDTH, BR_WIDTH), BR_WIDTH ** -0.5),
        'c_b_glu': nrm((DEPTH, BR_WIDTH), 0.02),
        'd_mu': uni((DEPTH, D_COLS), 0.0, 1.0),
        'd_w0': uni((DEPTH, BR_WIDTH), -6.0, -1.0),
        'd_w_lora': nrm((DEPTH, D_LORA_W, BR_WIDTH), 0.1),
        'd_a0': nrm((DEPTH, BR_WIDTH), 0.1),
        'd_a_lora': nrm((DEPTH, D_LORA_A, BR_WIDTH), 0.1),
        'd_g_lora': nrm((DEPTH, D_LORA_G, BR_WIDTH), D_LORA_G ** -0.5),
        'd_k_k': nrm((DEPTH, BR_WIDTH), 0.02, 0.85),
        'd_k_a': nrm((DEPTH, BR_WIDTH), 0.02, 1.0),
        'd_r_k': nrm((DEPTH, D_HEADS, D_HEAD_DIM), 0.02, -0.04),
        'd_gn_w': nrm((DEPTH, BR_WIDTH), 0.02, 1.0),
        'd_gn_b': nrm((DEPTH, BR_WIDTH), 0.02),
        'w_branch': nrm((DEPTH, N_BRANCH, BR_WIDTH, D_MODEL), BR_WIDTH ** -0.5),
        'w_out': nrm((DEPTH, D_MODEL, D_MODEL), D_MODEL ** -0.5),
        'norm2': nrm((DEPTH, D_MODEL), 0.02, 1.0),
        'w_ff1': nrm((DEPTH, D_MODEL, D_FF), D_MODEL ** -0.5),
        'w_ff2': nrm((DEPTH, D_FF, D_MODEL), D_FF ** -0.5),
    }


def reference(x_prompt, x_sample, cache_kv_a, state_conv, state_ssm, state_s5, state_shift, state_wkv,
              norm1, w_in, a_q_gain, a_k_gain,
              b_conv_w, b_conv_b, b_dt_bias, b_a_log, b_d, b_norm,
              c_a_re, c_a_im, c_log_step, c_b_re, c_b_im, c_c_re, c_c_im, c_d, c_w_glu, c_b_glu,
              d_mu, d_w0, d_w_lora, d_a0, d_a_lora, d_g_lora, d_k_k, d_k_a, d_r_k, d_gn_w, d_gn_b,
              w_branch, w_out, norm2, w_ff1, w_ff2):
    P = dict(norm1=norm1, w_in=w_in, a_q_gain=a_q_gain, a_k_gain=a_k_gain,
             b_conv_w=b_conv_w, b_conv_b=b_conv_b, b_dt_bias=b_dt_bias, b_a_log=b_a_log,
             b_d=b_d, b_norm=b_norm,
             c_a_re=c_a_re, c_a_im=c_a_im, c_log_step=c_log_step, c_b_re=c_b_re, c_b_im=c_b_im,
             c_c_re=c_c_re, c_c_im=c_c_im, c_d=c_d, c_w_glu=c_w_glu, c_b_glu=c_b_glu,
             d_mu=d_mu, d_w0=d_w0, d_w_lora=d_w_lora, d_a0=d_a0, d_a_lora=d_a_lora,
             d_g_lora=d_g_lora, d_k_k=d_k_k, d_k_a=d_k_a, d_r_k=d_r_k, d_gn_w=d_gn_w, d_gn_b=d_gn_b,
             w_branch=w_branch, w_out=w_out, norm2=norm2, w_ff1=w_ff1, w_ff2=w_ff2)
    f32 = jnp.float32
    bp = x_prompt.shape[0]
    y_prompt, y_sample = x_prompt, x_sample
    new_p, new_s = [], []
    for i in range(DEPTH):
        y_prompt, st = _layer(y_prompt, None,
                              jnp.zeros((bp, B_CONV - 1, B_CONV_DIM), f32),
                              jnp.zeros((bp, B_HEADS, B_HEAD_DIM, B_STATE), f32),
                              jnp.zeros((bp, C_NGROUPS, C_STATE, 2), f32),
                              jnp.zeros((bp, D_COLS), f32),
                              jnp.zeros((bp, D_HEADS, D_HEAD_DIM, D_HEAD_DIM), f32), P, i)
        new_p.append(st)
        y_sample, st = _layer(y_sample, cache_kv_a[i], state_conv[i], state_ssm[i], state_s5[i],
                              state_shift[i], state_wkv[i], P, i)
        new_s.append(st)
    kv_a_prompt, conv_prompt, ssm_prompt, s5_prompt, shift_prompt, wkv_prompt = [
        jnp.stack(z) for z in zip(*new_p)]
    kv_a_sample, conv_sample, ssm_sample, s5_sample, shift_sample, wkv_sample = [
        jnp.stack(z) for z in zip(*new_s)]
    return (y_prompt, y_sample, kv_a_prompt, kv_a_sample, conv_prompt, conv_sample,
            ssm_prompt, ssm_sample, s5_prompt, s5_sample, shift_prompt, shift_sample,
            wkv_prompt, wkv_sample)
```

```python
import functools

import jax
import jax.numpy as jnp
import numpy as np
from jax import lax
from jax.experimental import pallas as pl
from jax.experimental.pallas import tpu as pltpu

F32 = jnp.float32
BF16 = jnp.bfloat16
HI = lax.Precision.HIGHEST

D_MODEL = 2048
BR = 768
N_BRANCH = 4
HEAD = 64
NHEAD = BR // HEAD
A_WIN = 2048
A_BLK = 128
A_NKB = A_WIN // A_BLK + 1
A_PATTERNS = ((128, 1), (512, 4), (2048, 16))
B_GROUPS = 4
B_STATE = 128
B_CONV = 4
B_BC = B_GROUPS * B_STATE
B_CONV_DIM = BR + 2 * B_BC
C_GROUP = 16
C_NG = BR // C_GROUP
C_STATE = 64
C_NS = C_NG * C_STATE
C_Q = 8
C_LN = C_NS // C_Q
C_WIN = 256
D_LW, D_LA, D_LG = 64, 64, 128
D_LORA = D_LW + D_LA + D_LG
D_COLS = 3 * BR + D_LORA
D_FF = 4 * D_MODEL
NORM_EPS = 1e-6
RWKV_GN_EPS = 64e-5
NEG = -1e30

P_Q, P_K, P_V, P_Z, P_U, P_X, P_R, P_DK, P_DV = (BR * n for n in range(9))
P_DL = 9 * BR
P_BM = P_DL + D_LORA
P_CM = P_BM + B_BC
P_GATE = P_CM + B_BC
P_COLS = P_GATE + N_BRANCH * D_MODEL
VMEM_LIMIT = 56 * 1024 * 1024


def _cparams(*sem):
    return pltpu.CompilerParams(dimension_semantics=sem, vmem_limit_bytes=VMEM_LIMIT)


def _mm(a, b, prec=None):
    return lax.dot_general(a, b, (((1,), (0,)), ((), ())), precision=prec, preferred_element_type=F32)


def _mm_nt(a, b, prec=None):
    return lax.dot_general(a, b, (((1,), (1,)), ((), ())), precision=prec, preferred_element_type=F32)


def _mm_tn(a, b, prec=None):
    return lax.dot_general(a, b, (((0,), (0,)), ((), ())), precision=prec, preferred_element_type=F32)


def _bmm(a, b):
    return _mm(a.astype(BF16), b.astype(BF16))


def _bmm_nt(a, b):
    return _mm_nt(a.astype(BF16), b.astype(BF16))


def _bmm_tn(a, b):
    return _mm_tn(a.astype(BF16), b.astype(BF16))


def _iota(shape, dim):
    return lax.broadcasted_iota(jnp.int32, shape, dim)


def _sigmoid(x):
    return 1.0 / (1.0 + jnp.exp(-x))


def _softplus(x):
    return jnp.maximum(x, 0.0) + jnp.log(1.0 + jnp.exp(-jnp.abs(x)))


def _silu(x):
    return x * _sigmoid(x)


def _const_spec(a, ngrid):
    nd = a.ndim
    return pl.BlockSpec(a.shape, lambda *_: (0,) * nd)


def _norm_mm_kernel(x_ref, g_ref, w_ref, *rest, relu2, with_dt):
    if with_dt:
        wdt_ref, o_ref, odt_ref, h_scr = rest
    else:
        o_ref, h_scr = rest

    @pl.when(pl.program_id(1) == 0)
    def _():
        x = x_ref[...]
        ms = jnp.mean(x * x, axis=-1, keepdims=True)
        h = (x * lax.rsqrt(ms + NORM_EPS) * g_ref[...]).astype(BF16)
        h_scr[...] = h
        if with_dt:
            odt_ref[...] = _mm(h, wdt_ref[...])

    y = _mm(h_scr[...], w_ref[...])
    if relu2:
        y = jnp.square(jnp.maximum(y, 0.0))
    o_ref[...] = y.astype(o_ref.dtype)


def _norm_mm(x, g, w, wdt=None, *, tm, tn, relu2=False, out_dtype=F32):
    m, k = x.shape
    n = w.shape[1]
    with_dt = wdt is not None
    in_specs = [pl.BlockSpec((tm, k), lambda i, j: (i, 0)),
                pl.BlockSpec((1, k), lambda i, j: (0, 0)),
                pl.BlockSpec((k, tn), lambda i, j: (0, j))]
    out_specs = pl.BlockSpec((tm, tn), lambda i, j: (i, j))
    out_shape = jax.ShapeDtypeStruct((m, n), out_dtype)
    args = [x, g, w]
    if with_dt:
        in_specs.append(pl.BlockSpec((k, 128), lambda i, j: (0, 0)))
        out_specs = [out_specs, pl.BlockSpec((tm, 128), lambda i, j: (i, 0))]
        out_shape = [out_shape, jax.ShapeDtypeStruct((m, 128), F32)]
        args.append(wdt)
    return pl.pallas_call(
        functools.partial(_norm_mm_kernel, relu2=relu2, with_dt=with_dt),
        grid=(m // tm, n // tn), in_specs=in_specs, out_specs=out_specs, out_shape=out_shape,
        scratch_shapes=[pltpu.VMEM((tm, k), BF16)],
        compiler_params=_cparams("parallel", "arbitrary"),
    )(*args)


def _mm_res_kernel(a_ref, w_ref, r_ref, o_ref):
    @pl.when(pl.program_id(2) == 0)
    def _():
        o_ref[...] = r_ref[...]

    o_ref[...] += _mm(a_ref[...], w_ref[...])


def _mm_res(a, w, res, *, tm, tn, tk):
    m, k = a.shape
    n = w.shape[1]
    return pl.pallas_call(
        _mm_res_kernel, grid=(m // tm, n // tn, k // tk),
        in_specs=[pl.BlockSpec((tm, tk), lambda i, j, kk: (i, kk)),
                  pl.BlockSpec((tk, tn), lambda i, j, kk: (kk, j)),
                  pl.BlockSpec((tm, tn), lambda i, j, kk: (i, j))],
        out_specs=pl.BlockSpec((tm, tn), lambda i, j, kk: (i, j)),
        out_shape=jax.ShapeDtypeStruct((m, n), F32),
        compiler_params=_cparams("parallel", "parallel", "arbitrary"),
    )(a, w, res)


def _merge_kernel(oa, ob, oc, od, wb, g0, g1, g2, g3, o_ref):
    acc = None
    for n, (o, g) in enumerate(((oa, g0), (ob, g1), (oc, g2), (od, g3))):
        t = _sigmoid(g[...]) * _mm(o[...], wb[n])
        acc = t if acc is None else acc + t
    o_ref[...] = acc.astype(o_ref.dtype)


def _merge(outs, wb, proj, *, tm, tn):
    m = proj.shape[0]
    o_spec = pl.BlockSpec((tm, BR), lambda i, j: (i, 0))

    def g_spec(n):
        return pl.BlockSpec((tm, tn), lambda i, j: (i, (P_GATE + n * D_MODEL) // tn + j))

    return pl.pallas_call(
        _merge_kernel, grid=(m // tm, D_MODEL // tn),
        in_specs=[o_spec] * 4 + [pl.BlockSpec((N_BRANCH, BR, tn), lambda i, j: (0, 0, j))]
        + [g_spec(n) for n in range(N_BRANCH)],
        out_specs=pl.BlockSpec((tm, tn), lambda i, j: (i, j)),
        out_shape=jax.ShapeDtypeStruct((m, D_MODEL), BF16),
        compiler_params=_cparams("parallel", "parallel"),
    )(*outs, wb, proj, proj, proj, proj)


def _a_prep_kernel(q_ref, k_ref, v_ref, qg_ref, kg_ref, blk_ref, qn_ref, kv_ref):
    blk = blk_ref[...]
    q = q_ref[...]
    k = k_ref[...]
    qms = _mm(q * q, blk, HI) * (1.0 / HEAD)
    kms = _mm(k * k, blk, HI) * (1.0 / HEAD)
    qn_ref[...] = (q * lax.rsqrt(qms + NORM_EPS) * qg_ref[...] * (HEAD ** -0.5)).astype(BF16)
    kv_ref[:, :BR] = k * lax.rsqrt(kms + NORM_EPS) * kg_ref[...]
    kv_ref[:, BR:] = v_ref[...]


def _a_prep(proj, qg, kg, blk, *, tm):
    m = proj.shape[0]

    def col(c):
        return pl.BlockSpec((tm, BR), lambda i: (i, c))

    vec = pl.BlockSpec((1, BR), lambda i: (0, 0))
    return pl.pallas_call(
        _a_prep_kernel, grid=(m // tm,),
        in_specs=[col(P_Q // BR), col(P_K // BR), col(P_V // BR), vec, vec, pl.BlockSpec((BR, BR), lambda i: (0, 0))],
        out_specs=[pl.BlockSpec((tm, BR), lambda i: (i, 0)), pl.BlockSpec((tm, 2 * BR), lambda i: (i, 0))],
        out_shape=[jax.ShapeDtypeStruct((m, BR), BF16), jax.ShapeDtypeStruct((m, 2 * BR), F32)],
        compiler_params=_cparams("parallel"),
    )(proj, proj, proj, qg, kg, blk)


def _a_mult_table():
    kb = np.arange(A_NKB)[:, None, None]
    d = (A_NKB - 1 - kb) * A_BLK + np.arange(A_BLK)[None, :, None] - np.arange(A_BLK)[None, None, :]
    mult = np.zeros(d.shape, np.float32)
    for w, dil in A_PATTERNS:
        mult += ((d >= 0) & (d <= w) & (d % dil == 0)).astype(np.float32)
    return jnp.asarray(mult)


def _a_attn_kernel(q_ref, k_ref, v_ref, mt_ref, o_ref):
    i = pl.program_id(1)
    q = q_ref[...]
    lane = _iota((A_BLK, 2 * HEAD), 1)
    zero = jnp.zeros_like(q)
    qs = (jnp.where(lane < HEAD, q, zero), jnp.where(lane >= HEAD, q, zero))

    def body(kb, carry):
        start = pl.multiple_of((i - (A_NKB - 1) + kb) * A_BLK, A_BLK)
        kblk = k_ref[pl.ds(start, A_BLK), :].astype(BF16)
        vblk = v_ref[pl.ds(start, A_BLK), :].astype(BF16)
        mult = mt_ref[kb]
        new = []
        for h in range(2):
            m, l, acc = carry[3 * h:3 * h + 3]
            s = jnp.where(mult > 0.0, _mm_nt(qs[h], kblk), NEG)
            mn = jnp.maximum(m, jnp.max(s, axis=-1, keepdims=True))
            al = jnp.exp(m - mn)
            p = mult * jnp.exp(s - mn)
            new += [mn, al * l + jnp.sum(p, axis=-1, keepdims=True), al * acc + _mm(p.astype(BF16), vblk)]
        return tuple(new)

    init = (jnp.full((A_BLK, 1), NEG, F32), jnp.zeros((A_BLK, 1), F32), jnp.zeros((A_BLK, 2 * HEAD), F32)) * 2
    _, l0, a0, _, l1, a1 = lax.fori_loop(jnp.maximum(0, A_NKB - 1 - i), A_NKB, body, init)
    o_ref[...] = jnp.where(lane < HEAD, a0 / l0, a1 / l1).astype(o_ref.dtype)


def _a_attn_prompt(qn, kv, mult):
    l = qn.shape[0]
    npair = NHEAD // 2
    return pl.pallas_call(
        _a_attn_kernel, grid=(npair, l // A_BLK),
        in_specs=[pl.BlockSpec((A_BLK, 2 * HEAD), lambda hp, i: (i, hp)),
                  pl.BlockSpec((l, 2 * HEAD), lambda hp, i: (0, hp)),
                  pl.BlockSpec((l, 2 * HEAD), lambda hp, i: (0, npair + hp)),
                  pl.BlockSpec((A_NKB, A_BLK, A_BLK), lambda hp, i: (0, 0, 0))],
        out_specs=pl.BlockSpec((A_BLK, 2 * HEAD), lambda hp, i: (i, hp)),
        out_shape=jax.ShapeDtypeStruct((l, BR), BF16),
        compiler_params=_cparams("parallel", "arbitrary"),
    )(qn, kv, kv, mult)


A_T = 4
A_ND = 512
A_DIL3 = 16


def _a_sample_tables(past):
    ns = past // A_DIL3
    col_t = np.arange(64) // 16
    kr = np.arange(A_ND)[:, None]
    d = A_ND + col_t[None, :] - kr
    md = ((d >= 0) & (d <= 128)).astype(np.float32) + ((d % 4 == 0) & (d <= 512)).astype(np.float32)
    ms = np.stack([np.broadcast_to((col_t == c).astype(np.float32)[None, :], (ns, 64)) for c in range(A_T)])
    tk = np.arange(8)[:, None]
    mn = np.where((tk <= col_t[None, :]) & (tk < A_T), 1.0 + 2.0 * (tk == col_t[None, :]), 0.0).astype(np.float32)
    hm = np.zeros((16, BR), np.float32)
    for h in range(NHEAD):
        hm[h, h * HEAD:(h + 1) * HEAD] = 1.0
    return jnp.asarray(md), jnp.asarray(ms), jnp.asarray(mn), jnp.asarray(hm)


def _a_sample_kernel(q_ref, kvn_ref, dense_ref, strd_ref, hm_ref, md_ref, ms_ref, mn_ref, o_ref, new_scr):
    hm = hm_ref[...]
    q = q_ref[0].astype(F32)
    qexp = jnp.concatenate([jnp.broadcast_to(q[t:t + 1, :], (16, BR)) * hm for t in range(A_T)], axis=0)
    qexp = qexp.astype(BF16)
    new_scr[...] = jnp.zeros_like(new_scr)
    new_scr[0:A_T, :] = kvn_ref[0]
    segs = [(dense_ref[0, 0, :, :BR], dense_ref[0, 0, :, BR:], md_ref[...])]
    for c in range(A_T):
        base = c * 2 * BR
        segs.append((strd_ref[0, 0, :, base:base + BR], strd_ref[0, 0, :, base + BR:base + 2 * BR], ms_ref[c]))
    segs.append((new_scr[:, :BR], new_scr[:, BR:], mn_ref[...]))
    scores = []
    m = jnp.full((1, 64), NEG, F32)
    for kk, _, mult in segs:
        s = jnp.where(mult > 0.0, _mm_nt(kk.astype(BF16), qexp), NEG)
        scores.append(s)
        m = jnp.maximum(m, jnp.max(s, axis=0, keepdims=True))
    ps = [mult * jnp.exp(s - m) for s, (_, _, mult) in zip(scores, segs)]
    den = ps[0].sum(axis=0, keepdims=True)
    for p in ps[1:]:
        den = den + p.sum(axis=0, keepdims=True)
    inv = 1.0 / den
    r = None
    for p, (_, vv, _) in zip(ps, segs):
        t = _mm_tn((p * inv).astype(BF16), vv.astype(BF16))
        r = t if r is None else r + t
    for t in range(A_T):
        o_ref[0, t:t + 1, :] = jnp.sum(r[16 * t:16 * t + 16, :] * hm, axis=0, keepdims=True).astype(o_ref.dtype)


def _a_attn_sample(qn, kv_new, cache, layer, tables):
    depth, nb, past = cache.shape[0], cache.shape[1], cache.shape[2]
    md, ms, mn, hm = tables
    dense_view = cache.reshape(depth, nb, past, 2 * BR)
    strd_view = cache.reshape(depth, nb, past // A_DIL3, A_DIL3 * 2 * BR)
    return pl.pallas_call(
        _a_sample_kernel, grid=(nb,),
        in_specs=[pl.BlockSpec((1, A_T, BR), lambda b: (b, 0, 0)),
                  pl.BlockSpec((1, A_T, 2 * BR), lambda b: (b, 0, 0)),
                  pl.BlockSpec((1, 1, A_ND, 2 * BR), lambda b: (layer, b, past // A_ND - 1, 0)),
                  pl.BlockSpec((1, 1, past // A_DIL3, A_T * 2 * BR), lambda b: (layer, b, 0, 0)),
                  _const_spec(hm, 1), _const_spec(md, 1), _const_spec(ms, 1), _const_spec(mn, 1)],
        out_specs=pl.BlockSpec((1, A_T, BR), lambda b: (b, 0, 0)),
        out_shape=jax.ShapeDtypeStruct((nb, A_T, BR), BF16),
        scratch_shapes=[pltpu.VMEM((8, 2 * BR), F32)],
        compiler_params=_cparams("parallel"),
    )(qn.reshape(nb, A_T, BR), kv_new.reshape(nb, A_T, 2 * BR), dense_view, strd_view, hm, md, ms, mn)


def _b_kernel(z_ref, x_ref, bm_ref, cm_ref, dt_ref, conv0_ref, ssm0_ref, cw_ref, cb_ref, dtb_ref, a_ref, dsk_ref,
              nw_ref, sel_ref, gblk_ref, y_ref, conv_ref, ssm_ref, xp_scr, s_scr, pad_scr, *, q, qp):
    c = pl.program_id(1)

    @pl.when(c == 0)
    def _():
        xp_scr[...] = jnp.zeros_like(xp_scr)
        xp_scr[5:8, :] = conv0_ref[0, 0]
        s_scr[...] = ssm0_ref[0, 0]

    xp_scr[8:8 + q, 0:BR] = x_ref[0]
    xp_scr[8:8 + q, BR:BR + B_BC] = bm_ref[0]
    xp_scr[8:8 + q, BR + B_BC:] = cm_ref[0]
    cw = cw_ref[...]
    conv = (cb_ref[...] + cw[3:4] * xp_scr[8:8 + qp, :] + cw[2:3] * xp_scr[7:7 + qp, :]
            + cw[1:2] * xp_scr[6:6 + qp, :] + cw[0:1] * xp_scr[5:5 + qp, :])
    tail = xp_scr[5 + q:8 + q, :]
    xp_scr[5:8, :] = tail
    conv_ref[0] = tail
    u = _silu(conv)
    xs = u[:, :BR]
    dt = _softplus(dt_ref[0] + dtb_ref[...])
    z = z_ref[0]
    if q != qp:
        pad_scr[...] = jnp.zeros_like(pad_scr)
        pad_scr[0:q, :BR] = z
        pad_scr[0:q, BR:] = dt
        z = pad_scr[:, :BR]
        dt = pad_scr[:, BR:]
    adt = dt * a_ref[...]
    trib = _iota((qp, qp), 0) >= _iota((qp, qp), 1)
    acum = _mm(trib.astype(F32), adt, HI)
    acum_t = _mm_nt(sel_ref[...], acum, HI)
    alast = acum[qp - 1:qp, :]
    dend = jnp.exp(alast - acum)
    eacum = jnp.exp(acum)
    elast = jnp.exp(alast)
    ys = []
    hpg = NHEAD // B_GROUPS
    for g in range(B_GROUPS):
        bm = u[:, BR + g * B_STATE:BR + (g + 1) * B_STATE]
        cm = u[:, BR + B_BC + g * B_STATE:BR + B_BC + (g + 1) * B_STATE]
        gmat = _bmm_nt(cm, bm)
        for h in range(g * hpg, (g + 1) * hpg):
            xh = xs[:, h * HEAD:(h + 1) * HEAD]
            xdt = xh * dt[:, h:h + 1]
            lmat = jnp.where(trib, jnp.exp(acum[:, h:h + 1] - acum_t[h:h + 1, :]), 0.0)
            s_h = s_scr[h]
            ys.append(_bmm(gmat * lmat, xdt) + eacum[:, h:h + 1] * _bmm_nt(cm, s_h) + xh * dsk_ref[:, h:h + 1])
            s_scr[h] = elast[:, h:h + 1] * s_h + _bmm_tn(xdt * dend[:, h:h + 1], bm)
    y = jnp.concatenate(ys, axis=-1) * _silu(z)
    ms = _mm(y * y, gblk_ref[...], HI)
    y = y * lax.rsqrt(ms + NORM_EPS) * nw_ref[...]
    y_ref[0] = y[:q].astype(y_ref.dtype)
    ssm_ref[0] = s_scr[...]


def _mixer_b(proj3, dt3, conv0, ssm0, layer, prm, *, q):
    nb, l, _ = proj3.shape
    qp = max(q, 8)
    consts = [prm[k] for k in ('b_cw', 'b_cb', 'b_dtb', 'b_a', 'b_dsk', 'b_nw', 'b_sel', 'b_gblk')]
    return pl.pallas_call(
        functools.partial(_b_kernel, q=q, qp=qp), grid=(nb, l // q),
        in_specs=[pl.BlockSpec((1, q, BR), lambda b, c: (b, c, P_Z // BR)),
                  pl.BlockSpec((1, q, BR), lambda b, c: (b, c, P_X // BR)),
                  pl.BlockSpec((1, q, B_BC), lambda b, c: (b, c, P_BM // B_BC)),
                  pl.BlockSpec((1, q, B_BC), lambda b, c: (b, c, P_CM // B_BC)),
                  pl.BlockSpec((1, q, 128), lambda b, c: (b, c, 0)),
                  pl.BlockSpec((1, 1, B_CONV - 1, B_CONV_DIM), lambda b, c: (layer, b, 0, 0)),
                  pl.BlockSpec((1, 1, NHEAD, HEAD, B_STATE), lambda b, c: (layer, b, 0, 0, 0))]
        + [_const_spec(a, 2) for a in consts],
        out_specs=[pl.BlockSpec((1, q, BR), lambda b, c: (b, c, 0)),
                   pl.BlockSpec((1, B_CONV - 1, B_CONV_DIM), lambda b, c: (b, 0, 0)),
                   pl.BlockSpec((1, NHEAD, HEAD, B_STATE), lambda b, c: (b, 0, 0, 0))],
        out_shape=[jax.ShapeDtypeStruct((nb, l, BR), BF16),
                   jax.ShapeDtypeStruct((nb, B_CONV - 1, B_CONV_DIM), F32),
                   jax.ShapeDtypeStruct((nb, NHEAD, HEAD, B_STATE), F32)],
        scratch_shapes=[pltpu.VMEM((qp + 8, B_CONV_DIM), F32), pltpu.VMEM((NHEAD, HEAD, B_STATE), F32),
                        pltpu.VMEM((qp, BR + 128), F32)],
        compiler_params=_cparams("parallel", "arbitrary"),
    )(proj3, proj3, proj3, proj3, dt3, conv0, ssm0, *consts)


def _gelu_tanh(y):
    return 0.5 * y * (1.0 + jnp.tanh(0.7978845608028654 * (y + 0.044715 * (y * y * y))))


def _c_window(qq):
    first = qq * C_LN // C_STATE * C_GROUP
    return min(first // 128 * 128, BR - C_WIN)


def _c_prompt_kernel(u_ref, h0_ref, lb_ref, bq_ref, cq_ref, dsk_ref, wg_ref, bg_ref, o_ref, hf_ref,
                     bu_scr, hs_scr, h_scr, *, t):
    @pl.when(pl.program_id(0) == 0)
    def _():
        h_scr[...] = h0_ref[...]

    u = u_ref[...]
    ub = u.astype(BF16)
    ntile = 2 * C_LN // 128
    for qq in range(C_Q):
        ws = _c_window(qq)
        bu = _mm(ub[:, ws:ws + C_WIN], bq_ref[qq])
        for j in range(ntile):
            bu_scr[j, pl.ds(qq, t, stride=C_Q), :] = bu[:, j * 128:(j + 1) * 128]
    lb = [lb_ref[:, j * 128:(j + 1) * 128] for j in range(ntile)]
    half = ntile // 2

    def step(i, h):
        r0 = pl.multiple_of(i * C_Q, C_Q)
        new = []
        for j in range(half):
            new.append(lb[j] * h[j] - lb[half + j] * h[half + j] + bu_scr[j, pl.ds(r0, C_Q), :])
        for j in range(half):
            new.append(lb[j] * h[half + j] + lb[half + j] * h[j] + bu_scr[half + j, pl.ds(r0, C_Q), :])
        for j in range(ntile):
            hs_scr[j, pl.ds(r0, C_Q), :] = new[j]
        return tuple(new)

    h = lax.fori_loop(0, t, step, tuple(h_scr[:, j * 128:(j + 1) * 128] for j in range(ntile)), unroll=8)
    for j in range(ntile):
        h_scr[:, j * 128:(j + 1) * 128] = h[j]
    hf_ref[...] = h_scr[...]
    ytile = [None] * (BR // 128)
    for qq in range(C_Q):
        ws = _c_window(qq) // 128
        hq = jnp.concatenate([hs_scr[j, pl.ds(qq, t, stride=C_Q), :] for j in range(ntile)], axis=-1)
        yq = _mm(hq.astype(BF16), cq_ref[qq])
        for j in range(C_WIN // 128):
            part = yq[:, j * 128:(j + 1) * 128]
            ytile[ws + j] = part if ytile[ws + j] is None else ytile[ws + j] + part
    y = _gelu_tanh(jnp.concatenate(ytile, axis=-1) + dsk_ref[...] * u)
    o_ref[...] = (y * _sigmoid(_bmm(y, wg_ref[...]) + bg_ref[...])).astype(o_ref.dtype)


def _mixer_c_prompt(proj, h0, prm, *, t):
    l = proj.shape[0]
    consts = [prm[k] for k in ('c_lb8', 'c_bq', 'c_cq', 'c_dsk', 'c_wg', 'c_bg')]
    return pl.pallas_call(
        functools.partial(_c_prompt_kernel, t=t), grid=(l // t,),
        in_specs=[pl.BlockSpec((t, BR), lambda c: (c, P_U // BR)), _const_spec(h0, 1)]
        + [_const_spec(a, 1) for a in consts],
        out_specs=[pl.BlockSpec((t, BR), lambda c: (c, 0)), pl.BlockSpec((C_Q, 2 * C_LN), lambda c: (0, 0))],
        out_shape=[jax.ShapeDtypeStruct((l, BR), BF16), jax.ShapeDtypeStruct((C_Q, 2 * C_LN), F32)],
        scratch_shapes=[pltpu.VMEM((2 * C_LN // 128, C_Q * t, 128), F32),
                        pltpu.VMEM((2 * C_LN // 128, C_Q * t, 128), F32),
                        pltpu.VMEM((C_Q, 2 * C_LN), F32)],
        compiler_params=_cparams("arbitrary"),
    )(proj, h0, *consts)


def _c_sample_kernel(u_ref, hr0_ref, hi0_ref, lbr_ref, lbi_ref, bf_ref, cf_ref, dsk_ref, wg_ref, bg_ref,
                     o_ref, hr_ref, hi_ref, *, nt):
    hr = hr0_ref[...]
    hi = hi0_ref[...]
    lbr = lbr_ref[...]
    lbi = lbi_ref[...]
    for t in range(nt):
        u = u_ref[t]
        bu = _bmm(u, bf_ref[...])
        hr, hi = lbr * hr - lbi * hi + bu[:, :C_NS], lbr * hi + lbi * hr + bu[:, C_NS:]
        y = _bmm(hr, cf_ref[:C_NS, :]) + _bmm(hi, cf_ref[C_NS:, :]) + dsk_ref[...] * u
        y = _gelu_tanh(y)
        o_ref[t] = (y * _sigmoid(_bmm(y, wg_ref[...]) + bg_ref[...])).astype(o_ref.dtype)
    hr_ref[...] = hr
    hi_ref[...] = hi


def _mixer_c_sample(u_t, hr0, hi0, prm):
    nt, nb, _ = u_t.shape
    args = [u_t, hr0, hi0] + [prm[k] for k in ('c_lbr', 'c_lbi', 'c_bf', 'c_cf', 'c_dsk', 'c_wg', 'c_bg')]
    return pl.pallas_call(
        functools.partial(_c_sample_kernel, nt=nt), grid=(1,),
        in_specs=[_const_spec(a, 1) for a in args],
        out_specs=[pl.BlockSpec((nt, nb, BR), lambda c: (0, 0, 0)), pl.BlockSpec((nb, C_NS), lambda c: (0, 0)),
                   pl.BlockSpec((nb, C_NS), lambda c: (0, 0))],
        out_shape=[jax.ShapeDtypeStruct((nt, nb, BR), BF16), jax.ShapeDtypeStruct((nb, C_NS), F32),
                   jax.ShapeDtypeStruct((nb, C_NS), F32)],
        compiler_params=_cparams("arbitrary"),
    )(*args)


def _d_kernel(r_ref, k_ref, v_ref, l_ref, sh0_ref, wkv0_ref, mu_ref, w0_ref, a0_ref, kk_ref, ka_ref, rk_ref,
              gnw_ref, gnb_ref, wl_ref, al_ref, gl_ref, blk_ref, y_ref, sh_ref, wkv_ref, x_scr, ht_scr, *, t, tp):
    c = pl.program_id(1)

    @pl.when(c == 0)
    def _():
        x_scr[...] = jnp.zeros_like(x_scr)
        x_scr[7:8, :] = sh0_ref[0, 0]
        ht_scr[...] = wkv0_ref[0, 0]

    x_scr[8:8 + t, 0:BR] = r_ref[0]
    x_scr[8:8 + t, BR:2 * BR] = k_ref[0]
    x_scr[8:8 + t, 2 * BR:3 * BR] = v_ref[0]
    x_scr[8:8 + t, 3 * BR:] = l_ref[0]
    cur = x_scr[8:8 + tp, :]
    xm = cur + (x_scr[7:7 + tp, :] - cur) * mu_ref[...]
    last = x_scr[7 + t:8 + t, :]
    x_scr[7:8, :] = last
    sh_ref[0] = last
    r = xm[:, :BR]
    k = xm[:, BR:2 * BR]
    v = xm[:, 2 * BR:3 * BR]
    xl = xm[:, 3 * BR:]
    lw = -jnp.exp(-_softplus(-(w0_ref[...] + _bmm(jnp.tanh(xl), wl_ref[...]))) - 0.5)
    a = _sigmoid(a0_ref[...] + _bmm(xl, al_ref[...]))
    g = _bmm(_sigmoid(xl), gl_ref[...])
    blk = blk_ref[...]
    kkr = k * kk_ref[...]
    kk = kkr / jnp.maximum(jnp.sqrt(_mm(kkr * kkr, blk, HI)), 1e-12)
    k2 = k * (1.0 + (a - 1.0) * ka_ref[...])
    if t != tp:
        live = (_iota((tp, 1), 0) < t).astype(F32)
        r, k2, v, kk, lw = r * live, k2 * live, v * live, kk * live, lw * live
    bv = kk * a
    row = _iota((tp, tp), 0)
    col = _iota((tp, tp), 1)
    incl = row >= col
    strict = row > col
    eye = (row == col).astype(F32)
    cum = _mm(incl.astype(F32), lw, HI)
    cend = cum[tp - 1:tp, :]
    e_in = jnp.exp(cum)
    e_neg = jnp.exp(-cum)
    e_end = jnp.exp(cend - cum)
    rd = r * e_in
    kkd = kk * jnp.exp(cum - lw)
    ks = k2 * e_neg
    bs = bv * e_neg
    kse = k2 * e_end
    bse = bv * e_end
    dend = jnp.exp(cend)
    ys = []
    for h in range(NHEAD):
        sl = slice(h * HEAD, (h + 1) * HEAD)
        akb = jnp.where(strict, _mm_nt(kkd[:, sl], bs[:, sl], HI), 0.0)
        akk = jnp.where(strict, _mm_nt(kkd[:, sl], ks[:, sl], HI), 0.0)
        arb = jnp.where(incl, _mm_nt(rd[:, sl], bs[:, sl], HI), 0.0)
        ark = jnp.where(incl, _mm_nt(rd[:, sl], ks[:, sl], HI), 0.0)
        inv = eye - akb
        pw = _mm(akb, akb, HI)
        n = 2
        while n < tp:
            inv = inv + _mm(inv, pw, HI)
            n *= 2
            if n < tp:
                pw = _mm(pw, pw, HI)
        w1 = _mm(inv, kkd[:, sl], HI)
        w2 = _mm(inv, _mm(akk, v[:, sl], HI), HI)
        ht = ht_scr[h]
        u = _mm_nt(w1, ht, HI) + w2
        ys.append(_mm_nt(rd[:, sl], ht, HI) + _mm(ark, v[:, sl], HI) - _mm(arb, u, HI))
        ht_scr[h] = ht * dend[:, sl] + _mm_tn(v[:, sl], kse[:, sl], HI) - _mm_tn(u, bse[:, sl], HI)
    y = jnp.concatenate(ys, axis=-1)
    mean = _mm(y, blk, HI) * (1.0 / HEAD)
    d = y - mean
    var = _mm(d * d, blk, HI) * (1.0 / HEAD)
    yn = d * lax.rsqrt(var + RWKV_GN_EPS) * gnw_ref[...] + gnb_ref[...]
    bonus = _mm(r * k2 * rk_ref[...], blk, HI)
    y_ref[0] = (((yn + bonus * v) * g)[:t]).astype(y_ref.dtype)
    wkv_ref[0] = ht_scr[...]


def _mixer_d(proj3, shift0, wkv0, layer, prm, *, t):
    nb, l, _ = proj3.shape
    tp = max(t, 8)
    consts = [prm[k] for k in ('d_mu', 'd_w0', 'd_a0', 'd_kk', 'd_ka', 'd_rk', 'd_gnw', 'd_gnb',
                               'd_wl', 'd_al', 'd_gl', 'd_blk')]
    return pl.pallas_call(
        functools.partial(_d_kernel, t=t, tp=tp), grid=(nb, l // t),
        in_specs=[pl.BlockSpec((1, t, BR), lambda b, c: (b, c, P_R // BR)),
                  pl.BlockSpec((1, t, BR), lambda b, c: (b, c, P_DK // BR)),
                  pl.BlockSpec((1, t, BR), lambda b, c: (b, c, P_DV // BR)),
                  pl.BlockSpec((1, t, D_LORA), lambda b, c: (b, c, P_DL // D_LORA)),
                  pl.BlockSpec((1, 1, 1, D_COLS), lambda b, c: (layer, b, 0, 0)),
                  pl.BlockSpec((1, 1, NHEAD, HEAD, HEAD), lambda b, c: (layer, b, 0, 0, 0))]
        + [_const_spec(a, 2) for a in consts],
        out_specs=[pl.BlockSpec((1, t, BR), lambda b, c: (b, c, 0)),
                   pl.BlockSpec((1, 1, D_COLS), lambda b, c: (b, 0, 0)),
                   pl.BlockSpec((1, NHEAD, HEAD, HEAD), lambda b, c: (b, 0, 0, 0))],
        out_shape=[jax.ShapeDtypeStruct((nb, l, BR), BF16),
                   jax.ShapeDtypeStruct((nb, 1, D_COLS), F32),
                   jax.ShapeDtypeStruct((nb, NHEAD, HEAD, HEAD), F32)],
        scratch_shapes=[pltpu.VMEM((tp + 8, D_COLS), F32), pltpu.VMEM((NHEAD, HEAD, HEAD), F32)],
        compiler_params=_cparams("parallel", "arbitrary"),
    )(proj3, proj3, proj3, proj3, shift0, wkv0, *consts)


_IN_SPLITS = (BR, BR, BR, BR, BR, B_BC, B_BC, NHEAD, BR, BR, D_LW, BR, BR, D_LA, D_LG, N_BRANCH * D_MODEL)
_IN_NAMES = ('q', 'k', 'v', 'z', 'x', 'bm', 'cm', 'dt', 'u', 'r', 'wl', 'dk', 'dv', 'al', 'gl', 'gate')
_PACK_ORDER = ('q', 'k', 'v', 'z', 'u', 'x', 'r', 'dk', 'dv', 'gl', 'wl', 'al', 'bm', 'cm', 'gate')
_D_SPLITS = (BR, D_LW, BR, BR, D_LA, D_LG)
_D_PERM = np.concatenate([np.arange(o, o + n) for o, n in (
    (0, BR), (BR + D_LW, BR), (2 * BR + D_LW, BR), (3 * BR + D_LW + D_LA, D_LG), (BR, D_LW), (3 * BR + D_LW, D_LA))])
_D_INV = np.argsort(_D_PERM)


def _head_block(width, group):
    idx = np.arange(width) // group
    return jnp.asarray((idx[:, None] == idx[None, :]).astype(np.float32))


def _layer_params(i, P):
    prm = {}
    offs = np.cumsum((0,) + _IN_SPLITS)
    w = P['w_in'][i]
    cols = {n: w[:, offs[j]:offs[j + 1]] for j, n in enumerate(_IN_NAMES)}
    prm['w_in'] = jnp.concatenate([cols[n] for n in _PACK_ORDER], axis=1).astype(BF16)
    prm['w_dt'] = jnp.pad(cols['dt'], ((0, 0), (0, 128 - NHEAD))).astype(BF16)
    prm['norm1'] = P['norm1'][i][None]
    prm['norm2'] = P['norm2'][i][None]
    prm['w_branch'] = P['w_branch'][i].astype(BF16)
    prm['w_out'] = P['w_out'][i].astype(BF16)
    prm['w_ff1'] = P['w_ff1'][i].astype(BF16)
    prm['w_ff2'] = P['w_ff2'][i].astype(BF16)
    prm['a_qg'] = jnp.tile(P['a_q_gain'][i], NHEAD)[None]
    prm['a_kg'] = jnp.tile(P['a_k_gain'][i], NHEAD)[None]
    prm['blk64'] = _head_block(BR, HEAD)
    pad12 = lambda v: jnp.pad(v, (0, 128 - NHEAD))[None]
    prm['b_cw'] = P['b_conv_w'][i]
    prm['b_cb'] = P['b_conv_b'][i][None]
    prm['b_dtb'] = pad12(P['b_dt_bias'][i])
    prm['b_a'] = pad12(-jnp.exp(P['b_a_log'][i]))
    prm['b_dsk'] = pad12(P['b_d'][i])
    prm['b_nw'] = P['b_norm'][i][None]
    prm['b_sel'] = jnp.asarray(np.eye(16, 128, dtype=np.float32))
    prm['b_gblk'] = _head_block(BR, BR // B_GROUPS) * (B_GROUPS / BR)
    a_re, a_im = P['c_a_re'][i], P['c_a_im'][i]
    step = jnp.exp(P['c_log_step'][i])[:, None]
    mag = jnp.exp(a_re * step)
    lb_re, lb_im = mag * jnp.cos(a_im * step), mag * jnp.sin(a_im * step)
    den = a_re * a_re + a_im * a_im
    f_re = ((lb_re - 1.0) * a_re + lb_im * a_im) / den
    f_im = (lb_im * a_re - (lb_re - 1.0) * a_im) / den
    b_re, b_im = P['c_b_re'][i], P['c_b_im'][i]
    bb_re = f_re[..., None] * b_re - f_im[..., None] * b_im
    bb_im = f_re[..., None] * b_im + f_im[..., None] * b_re
    eye_g = jnp.eye(C_NG, dtype=F32)
    bfull_re = jnp.einsum('gpc,gh->gchp', bb_re, eye_g).reshape(BR, C_NS)
    bfull_im = jnp.einsum('gpc,gh->gchp', bb_im, eye_g).reshape(BR, C_NS)
    cfull_re = jnp.einsum('gcp,gh->gphc', P['c_c_re'][i], eye_g).reshape(C_NS, BR)
    cfull_im = jnp.einsum('gcp,gh->gphc', P['c_c_im'][i], eye_g).reshape(C_NS, BR)
    prm['c_lbr'] = lb_re.reshape(1, C_NS)
    prm['c_lbi'] = lb_im.reshape(1, C_NS)
    prm['c_lb8'] = jnp.concatenate([lb_re.reshape(C_Q, C_LN), lb_im.reshape(C_Q, C_LN)], axis=1)
    prm['c_bf'] = jnp.concatenate([bfull_re, bfull_im], axis=1).astype(BF16)
    prm['c_cf'] = jnp.concatenate([cfull_re, -cfull_im], axis=0).astype(BF16)
    bq, cq = [], []
    for qq in range(C_Q):
        ws, s0 = _c_window(qq), qq * C_LN
        bq.append(jnp.concatenate([bfull_re[ws:ws + C_WIN, s0:s0 + C_LN], bfull_im[ws:ws + C_WIN, s0:s0 + C_LN]], 1))
        cq.append(jnp.concatenate([cfull_re[s0:s0 + C_LN, ws:ws + C_WIN], -cfull_im[s0:s0 + C_LN, ws:ws + C_WIN]], 0))
    prm['c_bq'] = jnp.stack(bq).astype(BF16)
    prm['c_cq'] = jnp.stack(cq).astype(BF16)
    prm['c_dsk'] = P['c_d'][i][None]
    prm['c_wg'] = P['c_w_glu'][i].astype(BF16)
    prm['c_bg'] = P['c_b_glu'][i][None]
    prm['d_mu'] = P['d_mu'][i][_D_PERM][None]
    for k_, n_ in (('d_w0', 'd_w0'), ('d_a0', 'd_a0'), ('d_kk', 'd_k_k'), ('d_ka', 'd_k_a'),
                   ('d_gnw', 'd_gn_w'), ('d_gnb', 'd_gn_b')):
        prm[k_] = P[n_][i][None]
    prm['d_rk'] = P['d_r_k'][i].reshape(1, BR)
    zl = lambda r: jnp.zeros((r, BR), F32)
    prm['d_gl'] = jnp.concatenate([P['d_g_lora'][i], zl(D_LW + D_LA)], 0).astype(BF16)
    prm['d_wl'] = jnp.concatenate([zl(D_LG), P['d_w_lora'][i], zl(D_LA)], 0).astype(BF16)
    prm['d_al'] = jnp.concatenate([zl(D_LG + D_LW), P['d_a_lora'][i]], 0).astype(BF16)
    prm['d_blk'] = prm['blk64']
    return prm


def _layer(x, states, layer, prm, tabs, *, nb, prompt):
    m = x.shape[0]
    l = m // nb
    tm = min(m, 512)
    proj, dt = _norm_mm(x, prm['norm1'], prm['w_in'], prm['w_dt'], tm=tm, tn=1024)
    proj3 = proj.reshape(nb, l, P_COLS)
    dt3 = dt.reshape(nb, l, 128)
    qn, kv = _a_prep(proj, prm['a_qg'], prm['a_kg'], prm['blk64'], tm=tm)
    if prompt:
        oa = _a_attn_prompt(qn, kv, tabs['a_mult'])
        kv_new = kv[m - min(A_WIN, m):].reshape(nb, min(A_WIN, m), 2, NHEAD, HEAD)
    else:
        oa = _a_attn_sample(qn, kv, states['kv'], layer, tabs['a_sample']).reshape(m, BR)
        kv_new = kv.reshape(nb, l, 2, NHEAD, HEAD)
    ob, conv_new, ssm_new = _mixer_b(proj3, dt3, states['conv'], states['ssm'], layer, prm, q=min(l, 128))
    if prompt:
        oc, hf = _mixer_c_prompt(proj, states['s5'], prm, t=min(l, 256))
        s5_new = jnp.stack([hf[:, :C_LN].reshape(C_NG, C_STATE), hf[:, C_LN:].reshape(C_NG, C_STATE)], -1)[None]
    else:
        u_t = proj3[:, :, P_U:P_U + BR].transpose(1, 0, 2)
        s0 = states['s5'][layer]
        oc_t, hr, hi = _mixer_c_sample(u_t, s0[..., 0].reshape(nb, C_NS), s0[..., 1].reshape(nb, C_NS), prm)
        oc = oc_t.transpose(1, 0, 2).reshape(m, BR)
        s5_new = jnp.stack([hr.reshape(nb, C_NG, C_STATE), hi.reshape(nb, C_NG, C_STATE)], -1)
    od, sh_new, wkv_new = _mixer_d(proj3, states['shift'], states['wkv'], layer, prm, t=min(l, 64))
    shift_new = sh_new.reshape(nb, D_COLS)[:, _D_INV]
    merged = _merge([oa, ob.reshape(m, BR), oc, od.reshape(m, BR)], prm['w_branch'], proj, tm=tm, tn=512)
    x = _mm_res(merged, prm['w_out'], x, tm=tm, tn=1024, tk=D_MODEL)
    hid = _norm_mm(x, prm['norm2'], prm['w_ff1'], tm=tm, tn=1024, relu2=True, out_dtype=BF16)
    x = _mm_res(hid, prm['w_ff2'], x, tm=tm, tn=1024, tk=2048)
    return x, (kv_new, conv_new, ssm_new, s5_new, shift_new, wkv_new)


def kernel(x_prompt, x_sample, cache_kv_a, state_conv, state_ssm, state_s5, state_shift, state_wkv,
           norm1, w_in, a_q_gain, a_k_gain,
           b_conv_w, b_conv_b, b_dt_bias, b_a_log, b_d, b_norm,
           c_a_re, c_a_im, c_log_step, c_b_re, c_b_im, c_c_re, c_c_im, c_d, c_w_glu, c_b_glu,
           d_mu, d_w0, d_w_lora, d_a0, d_a_lora, d_g_lora, d_k_k, d_k_a, d_r_k, d_gn_w, d_gn_b,
           w_branch, w_out, norm2, w_ff1, w_ff2):
    P = dict(norm1=norm1, w_in=w_in, a_q_gain=a_q_gain, a_k_gain=a_k_gain,
             b_conv_w=b_conv_w, b_conv_b=b_conv_b, b_dt_bias=b_dt_bias, b_a_log=b_a_log,
             b_d=b_d, b_norm=b_norm,
             c_a_re=c_a_re, c_a_im=c_a_im, c_log_step=c_log_step, c_b_re=c_b_re, c_b_im=c_b_im,
             c_c_re=c_c_re, c_c_im=c_c_im, c_d=c_d, c_w_glu=c_w_glu, c_b_glu=c_b_glu,
             d_mu=d_mu, d_w0=d_w0, d_w_lora=d_w_lora, d_a0=d_a0, d_a_lora=d_a_lora,
             d_g_lora=d_g_lora, d_k_k=d_k_k, d_k_a=d_k_a, d_r_k=d_r_k, d_gn_w=d_gn_w, d_gn_b=d_gn_b,
             w_branch=w_branch, w_out=w_out, norm2=norm2, w_ff1=w_ff1, w_ff2=w_ff2)
    depth = w_in.shape[0]
    bp, lp, _ = x_prompt.shape
    bs, ls, _ = x_sample.shape
    assert bp == 1 and ls == A_T
    tabs = {'a_mult': _a_mult_table(), 'a_sample': _a_sample_tables(cache_kv_a.shape[2])}
    zero_states = {'conv': jnp.zeros((1, bp, B_CONV - 1, B_CONV_DIM), F32),
                   'ssm': jnp.zeros((1, bp, NHEAD, HEAD, B_STATE), F32),
                   's5': jnp.zeros((C_Q, 2 * C_LN), F32),
                   'shift': jnp.zeros((1, bp, 1, D_COLS), F32),
                   'wkv': jnp.zeros((1, bp, NHEAD, HEAD, HEAD), F32)}
    samp_states = {'kv': cache_kv_a, 'conv': state_conv, 'ssm': state_ssm, 's5': state_s5,
                   'shift': state_shift[:, :, _D_PERM][:, :, None, :], 'wkv': state_wkv}
    yp = x_prompt.reshape(bp * lp, D_MODEL)
    ys = x_sample.reshape(bs * ls, D_MODEL)
    new_p, new_s = [], []
    for i in range(depth):
        prm = _layer_params(i, P)
        yp, st = _layer(yp, zero_states, 0, prm, tabs, nb=bp, prompt=True)
        new_p.append(st)
        ys, st = _layer(ys, samp_states, i, prm, tabs, nb=bs, prompt=False)
        new_s.append(st)
    outs_p = [jnp.stack(z) for z in zip(*new_p)]
    outs_s = [jnp.stack(z) for z in zip(*new_s)]
    res = [yp.reshape(bp, lp, D_MODEL), ys.reshape(bs, ls, D_MODEL)]
    for a, b in zip(outs_p, outs_s):
        res += [a, b]
    return tuple(res)
```

```python
import functools

import jax
import jax.numpy as jnp
import numpy as np
from jax import lax
from jax.experimental import pallas as pl
from jax.experimental.pallas import tpu as pltpu

F32 = jnp.float32
BF16 = jnp.bfloat16
HI = lax.Precision.HIGHEST

D_MODEL = 2048
BR = 768
N_BRANCH = 4
HEAD = 64
NHEAD = BR // HEAD
A_WIN = 2048
A_BLK = 128
A_NKB = A_WIN // A_BLK + 1
A_PATTERNS = ((128, 1), (512, 4), (2048, 16))
B_GROUPS = 4
B_STATE = 128
B_CONV = 4
B_BC = B_GROUPS * B_STATE
B_CONV_DIM = BR + 2 * B_BC
C_GROUP = 16
C_NG = BR // C_GROUP
C_STATE = 64
C_NS = C_NG * C_STATE
C_Q = 8
C_LN = C_NS // C_Q
C_WIN = 256
D_LW, D_LA, D_LG = 64, 64, 128
D_LORA = D_LW + D_LA + D_LG
D_COLS = 3 * BR + D_LORA
D_FF = 4 * D_MODEL
NORM_EPS = 1e-6
RWKV_GN_EPS = 64e-5
NEG = -1e30

P_Q, P_K, P_V, P_Z, P_U, P_X, P_R, P_DK, P_DV = (BR * n for n in range(9))
P_DL = 9 * BR
P_BM = P_DL + D_LORA
P_CM = P_BM + B_BC
P_GATE = P_CM + B_BC
P_COLS = P_GATE + N_BRANCH * D_MODEL
VMEM_LIMIT = 56 * 1024 * 1024
D_PASSES = (1, 1, 3)


def _cparams(*sem):
    return pltpu.CompilerParams(dimension_semantics=sem, vmem_limit_bytes=VMEM_LIMIT)


def _mm(a, b, prec=None):
    return lax.dot_general(a, b, (((1,), (0,)), ((), ())), precision=prec, preferred_element_type=F32)


def _mm_nt(a, b, prec=None):
    return lax.dot_general(a, b, (((1,), (1,)), ((), ())), precision=prec, preferred_element_type=F32)


def _mm_tn(a, b, prec=None):
    return lax.dot_general(a, b, (((0,), (0,)), ((), ())), precision=prec, preferred_element_type=F32)


_DIMS = {'nn': (((1,), (0,)), ((), ())), 'nt': (((1,), (1,)), ((), ())), 'tn': (((0,), (0,)), ((), ()))}


def _split(x):
    hi = x.astype(BF16)
    return hi, (x - hi.astype(F32)).astype(BF16)


def _pdot(a, b, kind, passes):
    def dg(x, y, prec=None):
        return lax.dot_general(x, y, _DIMS[kind], precision=prec, preferred_element_type=F32)

    if passes == 6:
        return dg(a, b, HI)
    if passes == 1:
        return dg(a.astype(BF16), b.astype(BF16))
    ah, al = _split(a)
    bh, bl = _split(b)
    return dg(ah, bh) + (dg(ah, bl) + dg(al, bh))


def _sel_mm(x, sel_bf16, terms):
    acc = None
    for _ in range(terms):
        hi = x.astype(BF16)
        t = _mm(hi, sel_bf16)
        acc = t if acc is None else acc + t
        x = x - hi.astype(F32)
    return acc


def _bmm(a, b):
    return _mm(a.astype(BF16), b.astype(BF16))


def _bmm_nt(a, b):
    return _mm_nt(a.astype(BF16), b.astype(BF16))


def _bmm_tn(a, b):
    return _mm_tn(a.astype(BF16), b.astype(BF16))


def _iota(shape, dim):
    return lax.broadcasted_iota(jnp.int32, shape, dim)


def _sigmoid(x):
    return 1.0 / (1.0 + jnp.exp(-x))


def _softplus(x):
    return jnp.maximum(x, 0.0) + jnp.log(1.0 + jnp.exp(-jnp.abs(x)))


def _silu(x):
    return x * _sigmoid(x)


def _const_spec(a, ngrid):
    nd = a.ndim
    return pl.BlockSpec(a.shape, lambda *_: (0,) * nd)


def _norm_mm_kernel(x_ref, g_ref, w_ref, *rest, relu2, with_dt):
    if with_dt:
        wdt_ref, o_ref, odt_ref, h_scr = rest
    else:
        o_ref, h_scr = rest

    @pl.when(pl.program_id(1) == 0)
    def _():
        x = x_ref[...]
        ms = jnp.mean(x * x, axis=-1, keepdims=True)
        h = (x * lax.rsqrt(ms + NORM_EPS) * g_ref[...]).astype(BF16)
        h_scr[...] = h
        if with_dt:
            odt_ref[...] = _mm(h, wdt_ref[...])

    y = _mm(h_scr[...], w_ref[...])
    if relu2:
        y = jnp.square(jnp.maximum(y, 0.0))
    o_ref[...] = y.astype(o_ref.dtype)


def _norm_mm(x, g, w, wdt=None, *, tm, tn, relu2=False, out_dtype=F32):
    m, k = x.shape
    n = w.shape[1]
    with_dt = wdt is not None
    in_specs = [pl.BlockSpec((tm, k), lambda i, j: (i, 0)),
                pl.BlockSpec((1, k), lambda i, j: (0, 0)),
                pl.BlockSpec((k, tn), lambda i, j: (0, j))]
    out_specs = pl.BlockSpec((tm, tn), lambda i, j: (i, j))
    out_shape = jax.ShapeDtypeStruct((m, n), out_dtype)
    args = [x, g, w]
    if with_dt:
        in_specs.append(pl.BlockSpec((k, 128), lambda i, j: (0, 0)))
        out_specs = [out_specs, pl.BlockSpec((tm, 128), lambda i, j: (i, 0))]
        out_shape = [out_shape, jax.ShapeDtypeStruct((m, 128), F32)]
        args.append(wdt)
    return pl.pallas_call(
        functools.partial(_norm_mm_kernel, relu2=relu2, with_dt=with_dt),
        grid=(m // tm, n // tn), in_specs=in_specs, out_specs=out_specs, out_shape=out_shape,
        scratch_shapes=[pltpu.VMEM((tm, k), BF16)],
        compiler_params=_cparams("parallel", "arbitrary"),
    )(*args)


def _mm_res_kernel(a_ref, w_ref, r_ref, o_ref):
    @pl.when(pl.program_id(2) == 0)
    def _():
        o_ref[...] = r_ref[...]

    o_ref[...] += _mm(a_ref[...], w_ref[...])


def _mm_res(a, w, res, *, tm, tn, tk):
    m, k = a.shape
    n = w.shape[1]
    return pl.pallas_call(
        _mm_res_kernel, grid=(m // tm, n // tn, k // tk),
        in_specs=[pl.BlockSpec((tm, tk), lambda i, j, kk: (i, kk)),
                  pl.BlockSpec((tk, tn), lambda i, j, kk: (kk, j)),
                  pl.BlockSpec((tm, tn), lambda i, j, kk: (i, j))],
        out_specs=pl.BlockSpec((tm, tn), lambda i, j, kk: (i, j)),
        out_shape=jax.ShapeDtypeStruct((m, n), F32),
        compiler_params=_cparams("parallel", "parallel", "arbitrary"),
    )(a, w, res)


def _merge_kernel(oa, ob, oc, od, wb, g0, g1, g2, g3, o_ref):
    acc = None
    for n, (o, g) in enumerate(((oa, g0), (ob, g1), (oc, g2), (od, g3))):
        t = _sigmoid(g[...]) * _mm(o[...], wb[n])
        acc = t if acc is None else acc + t
    o_ref[...] = acc.astype(o_ref.dtype)


def _merge(outs, wb, proj, *, tm, tn):
    m = proj.shape[0]
    o_spec = pl.BlockSpec((tm, BR), lambda i, j: (i, 0))

    def g_spec(n):
        return pl.BlockSpec((tm, tn), lambda i, j: (i, (P_GATE + n * D_MODEL) // tn + j))

    return pl.pallas_call(
        _merge_kernel, grid=(m // tm, D_MODEL // tn),
        in_specs=[o_spec] * 4 + [pl.BlockSpec((N_BRANCH, BR, tn), lambda i, j: (0, 0, j))]
        + [g_spec(n) for n in range(N_BRANCH)],
        out_specs=pl.BlockSpec((tm, tn), lambda i, j: (i, j)),
        out_shape=jax.ShapeDtypeStruct((m, D_MODEL), BF16),
        compiler_params=_cparams("parallel", "parallel"),
    )(*outs, wb, proj, proj, proj, proj)


def _a_prep_kernel(q_ref, k_ref, v_ref, qg_ref, kg_ref, blk_ref, qn_ref, kv_ref):
    blk = blk_ref[...]
    q = q_ref[...]
    k = k_ref[...]
    qms = _sel_mm(q * q, blk, 3) * (1.0 / HEAD)
    kms = _sel_mm(k * k, blk, 3) * (1.0 / HEAD)
    qn_ref[...] = q * lax.rsqrt(qms + NORM_EPS) * qg_ref[...] * (HEAD ** -0.5)
    kv_ref[:, :BR] = k * lax.rsqrt(kms + NORM_EPS) * kg_ref[...]
    kv_ref[:, BR:] = v_ref[...]


def _a_prep(proj, qg, kg, blk, *, tm):
    m = proj.shape[0]

    def col(c):
        return pl.BlockSpec((tm, BR), lambda i: (i, c))

    vec = pl.BlockSpec((1, BR), lambda i: (0, 0))
    return pl.pallas_call(
        _a_prep_kernel, grid=(m // tm,),
        in_specs=[col(P_Q // BR), col(P_K // BR), col(P_V // BR), vec, vec, pl.BlockSpec((BR, BR), lambda i: (0, 0))],
        out_specs=[pl.BlockSpec((tm, BR), lambda i: (i, 0)), pl.BlockSpec((tm, 2 * BR), lambda i: (i, 0))],
        out_shape=[jax.ShapeDtypeStruct((m, BR), F32), jax.ShapeDtypeStruct((m, 2 * BR), F32)],
        compiler_params=_cparams("parallel"),
    )(proj, proj, proj, qg, kg, blk)


def _a_band_bias():
    a = np.arange(A_BLK)[:, None]
    c = np.arange(2 * A_BLK)[None, :]
    band = (c >= a) & (c <= a + A_BLK)
    return jnp.asarray(np.stack([np.where(band, 0.0, NEG), np.where(band & (c >= A_BLK), 0.0, NEG)]).astype(np.float32))


def _a_attn_kernel(q_ref, kp_ref, kc_ref, vp_ref, vc_ref, bias_ref, o_ref, k_scr, v_scr, m_scr, l_scr, acc_scr,
                   *, sb_len):
    sb = pl.program_id(1)
    k_scr[0:sb_len, :] = kp_ref[...]
    k_scr[sb_len:, :] = kc_ref[...]
    v_scr[0:sb_len, :] = vp_ref[...]
    v_scr[sb_len:, :] = vc_ref[...]
    lane = _iota((A_BLK, 2 * HEAD), 1)
    left = lane < HEAD
    first_sb = sb == 0
    for pi_, (w, d) in enumerate(A_PATTERNS):
        assert w // d == A_BLK and sb_len % (A_BLK * d) == 0
        for j in range(sb_len // (A_BLK * d)):
            for r in range(d):
                q0 = r + d * A_BLK * j
                k0 = sb_len + q0 - d * A_BLK
                rows_q = pl.ds(q0, A_BLK, stride=d) if d > 1 else pl.ds(q0, A_BLK)
                rows_k = pl.ds(k0, 2 * A_BLK, stride=d) if d > 1 else pl.ds(k0, 2 * A_BLK)
                q = q_ref[rows_q, :]
                kt = k_scr[rows_k, :].astype(BF16)
                vt = v_scr[rows_k, :].astype(BF16)
                if j == 0:
                    bias = jnp.where(first_sb, bias_ref[1], bias_ref[0])
                else:
                    bias = bias_ref[0]
                ms, ls, accs = [], [], []
                for h in range(2):
                    qh = jnp.where(left if h == 0 else ~left, q, 0.0).astype(BF16)
                    s = _mm_nt(qh, kt) + bias
                    mu = jnp.max(s, axis=-1, keepdims=True)
                    p = jnp.exp(s - mu)
                    ms.append(mu)
                    ls.append(jnp.sum(p, axis=-1, keepdims=True))
                    accs.append(_mm(p.astype(BF16), vt))
                mu = jnp.where(left, ms[0], ms[1])
                lu = jnp.where(left, ls[0], ls[1])
                au = jnp.where(left, accs[0], accs[1])
                if pi_ == 0:
                    m_scr[rows_q, :] = mu
                    l_scr[rows_q, :] = lu
                    acc_scr[rows_q, :] = au
                else:
                    mo = m_scr[rows_q, :]
                    mn = jnp.maximum(mo, mu)
                    eo = jnp.exp(mo - mn)
                    eu = jnp.exp(mu - mn)
                    m_scr[rows_q, :] = mn
                    l_scr[rows_q, :] = eo * l_scr[rows_q, :] + eu * lu
                    acc_scr[rows_q, :] = eo * acc_scr[rows_q, :] + eu * au
    o_ref[...] = (acc_scr[...] / l_scr[...]).astype(o_ref.dtype)


def _a_attn_prompt(qn, kv, bias):
    l = qn.shape[0]
    npair = NHEAD // 2
    sb_len = min(l, A_WIN)
    blk = (sb_len, 2 * HEAD)
    prev = lambda hp, s: (jnp.maximum(s - 1, 0), hp)
    prev_v = lambda hp, s: (jnp.maximum(s - 1, 0), npair + hp)
    return pl.pallas_call(
        functools.partial(_a_attn_kernel, sb_len=sb_len), grid=(npair, l // sb_len),
        in_specs=[pl.BlockSpec(blk, lambda hp, s: (s, hp)),
                  pl.BlockSpec(blk, prev), pl.BlockSpec(blk, lambda hp, s: (s, hp)),
                  pl.BlockSpec(blk, prev_v), pl.BlockSpec(blk, lambda hp, s: (s, npair + hp)),
                  _const_spec(bias, 2)],
        out_specs=pl.BlockSpec(blk, lambda hp, s: (s, hp)),
        out_shape=jax.ShapeDtypeStruct((l, BR), BF16),
        scratch_shapes=[pltpu.VMEM((2 * sb_len, 2 * HEAD), F32), pltpu.VMEM((2 * sb_len, 2 * HEAD), F32)]
        + [pltpu.VMEM(blk, F32)] * 3,
        compiler_params=_cparams("parallel", "arbitrary"),
    )(qn, kv, kv, kv, kv, bias)


A_T = 4


def _a_sample_tables(past):
    row_t = np.arange(64)[:, None] // 16
    d = past + row_t - np.arange(past)[None, :]
    mc = np.zeros(d.shape, np.float32)
    for w, dil in A_PATTERNS:
        mc += ((d >= 0) & (d <= w) & (d % dil == 0)).astype(np.float32)
    tk = np.arange(8)[None, :]
    mn = np.where((tk <= row_t) & (tk < A_T), 1.0 + (len(A_PATTERNS) - 1.0) * (tk == row_t), 0.0).astype(np.float32)
    hm = np.zeros((16, BR), np.float32)
    for h in range(NHEAD):
        hm[h, h * HEAD:(h + 1) * HEAD] = 1.0
    return jnp.asarray(mc), jnp.asarray(mn), jnp.asarray(hm)


def _a_sample_kernel(q_ref, kvn_ref, c_ref, hm_ref, mc_ref, mn_ref, o_ref, new_scr, *, past):
    hm = hm_ref[...]
    q = q_ref[0]
    qexp = jnp.concatenate([jnp.broadcast_to(q[t:t + 1, :], (16, BR)) * hm for t in range(A_T)], axis=0)
    qexp = qexp.astype(BF16)
    new_scr[...] = jnp.zeros_like(new_scr)
    new_scr[0:A_T, :] = kvn_ref[0]
    kt = c_ref[0, 0, 0].reshape(BR, past).astype(BF16)
    vt = c_ref[0, 0, 1].reshape(BR, past).astype(BF16)
    mc = mc_ref[...]
    mn = mn_ref[...]
    s = jnp.where(mc > 0.0, _mm(qexp, kt), NEG)
    sn = jnp.where(mn > 0.0, _mm_nt(qexp, new_scr[:, :BR].astype(BF16)), NEG)
    m = jnp.maximum(jnp.max(s, axis=-1, keepdims=True), jnp.max(sn, axis=-1, keepdims=True))
    p = mc * jnp.exp(s - m)
    pn = mn * jnp.exp(sn - m)
    den = jnp.sum(p, axis=-1, keepdims=True) + jnp.sum(pn, axis=-1, keepdims=True)
    r = (_mm_nt(p.astype(BF16), vt) + _mm(pn.astype(BF16), new_scr[:, BR:].astype(BF16))) / den
    for t in range(A_T):
        o_ref[0, t:t + 1, :] = jnp.sum(r[16 * t:16 * t + 16, :] * hm, axis=0, keepdims=True).astype(o_ref.dtype)


def _a_attn_sample(qn, kv_new, cache_t, layer, tables):
    nb, past = cache_t.shape[1], cache_t.shape[-1]
    mc, mn, hm = tables
    return pl.pallas_call(
        functools.partial(_a_sample_kernel, past=past), grid=(nb,),
        in_specs=[pl.BlockSpec((1, A_T, BR), lambda b: (b, 0, 0)),
                  pl.BlockSpec((1, A_T, 2 * BR), lambda b: (b, 0, 0)),
                  pl.BlockSpec((1, 1, 2, NHEAD, HEAD, past), lambda b: (layer, b, 0, 0, 0, 0)),
                  _const_spec(hm, 1), _const_spec(mc, 1), _const_spec(mn, 1)],
        out_specs=pl.BlockSpec((1, A_T, BR), lambda b: (b, 0, 0)),
        out_shape=jax.ShapeDtypeStruct((nb, A_T, BR), BF16),
        scratch_shapes=[pltpu.VMEM((8, 2 * BR), F32)],
        compiler_params=_cparams("parallel"),
    )(qn.reshape(nb, A_T, BR), kv_new.reshape(nb, A_T, 2 * BR), cache_t, hm, mc, mn)


def _b_kernel(z_ref, x_ref, bm_ref, cm_ref, dt_ref, conv0_ref, ssm0_ref, cw_ref, cb_ref, dtb_ref, a_ref, dsk_ref,
              nw_ref, sel_ref, gblk_ref, y_ref, conv_ref, ssm_ref, xp_scr, s_scr, pad_scr, *, q, qp):
    c = pl.program_id(1)

    @pl.when(c == 0)
    def _():
        xp_scr[...] = jnp.zeros_like(xp_scr)
        xp_scr[5:8, :] = conv0_ref[0, 0]
        s_scr[...] = ssm0_ref[0, 0]

    xp_scr[8:8 + q, 0:BR] = x_ref[0]
    xp_scr[8:8 + q, BR:BR + B_BC] = bm_ref[0]
    xp_scr[8:8 + q, BR + B_BC:] = cm_ref[0]
    cw = cw_ref[...]
    conv = (cb_ref[...] + cw[3:4] * xp_scr[8:8 + qp, :] + cw[2:3] * xp_scr[7:7 + qp, :]
            + cw[1:2] * xp_scr[6:6 + qp, :] + cw[0:1] * xp_scr[5:5 + qp, :])
    tail = xp_scr[5 + q:8 + q, :]
    xp_scr[5:8, :] = tail
    conv_ref[0] = tail
    u = _silu(conv)
    xs = u[:, :BR]
    dt = _softplus(dt_ref[0] + dtb_ref[...])
    z = z_ref[0]
    if q != qp:
        pad_scr[...] = jnp.zeros_like(pad_scr)
        pad_scr[0:q, :BR] = z
        pad_scr[0:q, BR:] = dt
        z = pad_scr[:, :BR]
        dt = pad_scr[:, BR:]
    adt = dt * a_ref[...]
    trib = _iota((qp, qp), 0) >= _iota((qp, qp), 1)
    acum = _mm(trib.astype(F32), adt, HI)
    acum_t = _mm_nt(sel_ref[...], acum, HI)
    alast = acum[qp - 1:qp, :]
    dend = jnp.exp(alast - acum)
    eacum = jnp.exp(acum)
    elast = jnp.exp(alast)
    ys = []
    hpg = NHEAD // B_GROUPS
    for g in range(B_GROUPS):
        bm = u[:, BR + g * B_STATE:BR + (g + 1) * B_STATE]
        cm = u[:, BR + B_BC + g * B_STATE:BR + B_BC + (g + 1) * B_STATE]
        gmat = _bmm_nt(cm, bm)
        for h in range(g * hpg, (g + 1) * hpg):
            xh = xs[:, h * HEAD:(h + 1) * HEAD]
            xdt = xh * dt[:, h:h + 1]
            lmat = jnp.where(trib, jnp.exp(acum[:, h:h + 1] - acum_t[h:h + 1, :]), 0.0)
            s_h = s_scr[h]
            ys.append(_bmm(gmat * lmat, xdt) + eacum[:, h:h + 1] * _bmm_nt(cm, s_h) + xh * dsk_ref[:, h:h + 1])
            s_scr[h] = elast[:, h:h + 1] * s_h + _bmm_tn(xdt * dend[:, h:h + 1], bm)
    y = jnp.concatenate(ys, axis=-1) * _silu(z)
    ms = _mm(y * y, gblk_ref[...], HI)
    y = y * lax.rsqrt(ms + NORM_EPS) * nw_ref[...]
    y_ref[0] = y[:q].astype(y_ref.dtype)
    ssm_ref[0] = s_scr[...]


def _mixer_b(proj3, dt3, conv0, ssm0, layer, prm, *, q):
    nb, l, _ = proj3.shape
    qp = max(q, 8)
    consts = [prm[k] for k in ('b_cw', 'b_cb', 'b_dtb', 'b_a', 'b_dsk', 'b_nw', 'b_sel', 'b_gblk')]
    return pl.pallas_call(
        functools.partial(_b_kernel, q=q, qp=qp), grid=(nb, l // q),
        in_specs=[pl.BlockSpec((1, q, BR), lambda b, c: (b, c, P_Z // BR)),
                  pl.BlockSpec((1, q, BR), lambda b, c: (b, c, P_X // BR)),
                  pl.BlockSpec((1, q, B_BC), lambda b, c: (b, c, P_BM // B_BC)),
                  pl.BlockSpec((1, q, B_BC), lambda b, c: (b, c, P_CM // B_BC)),
                  pl.BlockSpec((1, q, 128), lambda b, c: (b, c, 0)),
                  pl.BlockSpec((1, 1, B_CONV - 1, B_CONV_DIM), lambda b, c: (layer, b, 0, 0)),
                  pl.BlockSpec((1, 1, NHEAD, HEAD, B_STATE), lambda b, c: (layer, b, 0, 0, 0))]
        + [_const_spec(a, 2) for a in consts],
        out_specs=[pl.BlockSpec((1, q, BR), lambda b, c: (b, c, 0)),
                   pl.BlockSpec((1, B_CONV - 1, B_CONV_DIM), lambda b, c: (b, 0, 0)),
                   pl.BlockSpec((1, NHEAD, HEAD, B_STATE), lambda b, c: (b, 0, 0, 0))],
        out_shape=[jax.ShapeDtypeStruct((nb, l, BR), BF16),
                   jax.ShapeDtypeStruct((nb, B_CONV - 1, B_CONV_DIM), F32),
                   jax.ShapeDtypeStruct((nb, NHEAD, HEAD, B_STATE), F32)],
        scratch_shapes=[pltpu.VMEM((qp + 8, B_CONV_DIM), F32), pltpu.VMEM((NHEAD, HEAD, B_STATE), F32),
                        pltpu.VMEM((qp, BR + 128), F32)],
        compiler_params=_cparams("parallel", "arbitrary"),
    )(proj3, proj3, proj3, proj3, dt3, conv0, ssm0, *consts)


def _gelu_tanh(y):
    return 0.5 * y * (1.0 + jnp.tanh(0.7978845608028654 * (y + 0.044715 * (y * y * y))))


def _c_window(qq):
    first = qq * C_LN // C_STATE * C_GROUP
    return min(first // 128 * 128, BR - C_WIN)


def _c_prompt_kernel(u_ref, h0_ref, lb_ref, bq_ref, cq_ref, dsk_ref, wg_ref, bg_ref, o_ref, hf_ref,
                     bu_scr, hs_scr, h_scr, *, t):
    @pl.when(pl.program_id(0) == 0)
    def _():
        h_scr[...] = h0_ref[...]

    u = u_ref[...]
    ub = u.astype(BF16)
    ntile = 2 * C_LN // 128
    for qq in range(C_Q):
        ws = _c_window(qq)
        bu = _mm(ub[:, ws:ws + C_WIN], bq_ref[qq])
        for j in range(ntile):
            bu_scr[j, pl.ds(qq, t, stride=C_Q), :] = bu[:, j * 128:(j + 1) * 128]
    lb = [lb_ref[:, j * 128:(j + 1) * 128] for j in range(ntile)]
    half = ntile // 2

    def step(i, h):
        r0 = pl.multiple_of(i * C_Q, C_Q)
        new = []
        for j in range(half):
            new.append(lb[j] * h[j] - lb[half + j] * h[half + j] + bu_scr[j, pl.ds(r0, C_Q), :])
        for j in range(half):
            new.append(lb[j] * h[half + j] + lb[half + j] * h[j] + bu_scr[half + j, pl.ds(r0, C_Q), :])
        for j in range(ntile):
            hs_scr[j, pl.ds(r0, C_Q), :] = new[j]
        return tuple(new)

    h = lax.fori_loop(0, t, step, tuple(h_scr[:, j * 128:(j + 1) * 128] for j in range(ntile)), unroll=8)
    for j in range(ntile):
        h_scr[:, j * 128:(j + 1) * 128] = h[j]
    hf_ref[...] = h_scr[...]
    ytile = [None] * (BR // 128)
    for qq in range(C_Q):
        ws = _c_window(qq) // 128
        hq = jnp.concatenate([hs_scr[j, pl.ds(qq, t, stride=C_Q), :] for j in range(ntile)], axis=-1)
        yq = _mm(hq.astype(BF16), cq_ref[qq])
        for j in range(C_WIN // 128):
            part = yq[:, j * 128:(j + 1) * 128]
            ytile[ws + j] = part if ytile[ws + j] is None else ytile[ws + j] + part
    y = _gelu_tanh(jnp.concatenate(ytile, axis=-1) + dsk_ref[...] * u)
    o_ref[...] = (y * _sigmoid(_bmm(y, wg_ref[...]) + bg_ref[...])).astype(o_ref.dtype)


def _mixer_c_prompt(proj, h0, prm, *, t):
    l = proj.shape[0]
    consts = [prm[k] for k in ('c_lb8', 'c_bq', 'c_cq', 'c_dsk', 'c_wg', 'c_bg')]
    return pl.pallas_call(
        functools.partial(_c_prompt_kernel, t=t), grid=(l // t,),
        in_specs=[pl.BlockSpec((t, BR), lambda c: (c, P_U // BR)), _const_spec(h0, 1)]
        + [_const_spec(a, 1) for a in consts],
        out_specs=[pl.BlockSpec((t, BR), lambda c: (c, 0)), pl.BlockSpec((C_Q, 2 * C_LN), lambda c: (0, 0))],
        out_shape=[jax.ShapeDtypeStruct((l, BR), BF16), jax.ShapeDtypeStruct((C_Q, 2 * C_LN), F32)],
        scratch_shapes=[pltpu.VMEM((2 * C_LN // 128, C_Q * t, 128), F32),
                        pltpu.VMEM((2 * C_LN // 128, C_Q * t, 128), F32),
                        pltpu.VMEM((C_Q, 2 * C_LN), F32)],
        compiler_params=_cparams("arbitrary"),
    )(proj, h0, *consts)


def _c_sample_kernel(u_ref, hr0_ref, hi0_ref, lbr_ref, lbi_ref, bf_ref, cf_ref, dsk_ref, wg_ref, bg_ref,
                     o_ref, hr_ref, hi_ref, *, nt):
    hr = hr0_ref[...]
    hi = hi0_ref[...]
    lbr = lbr_ref[...]
    lbi = lbi_ref[...]
    for t in range(nt):
        u = u_ref[t]
        bu = _bmm(u, bf_ref[...])
        hr, hi = lbr * hr - lbi * hi + bu[:, :C_NS], lbr * hi + lbi * hr + bu[:, C_NS:]
        y = _bmm(hr, cf_ref[:C_NS, :]) + _bmm(hi, cf_ref[C_NS:, :]) + dsk_ref[...] * u
        y = _gelu_tanh(y)
        o_ref[t] = (y * _sigmoid(_bmm(y, wg_ref[...]) + bg_ref[...])).astype(o_ref.dtype)
    hr_ref[...] = hr
    hi_ref[...] = hi


def _mixer_c_sample(u_t, hr0, hi0, prm):
    nt, nb, _ = u_t.shape
    args = [u_t, hr0, hi0] + [prm[k] for k in ('c_lbr', 'c_lbi', 'c_bf', 'c_cf', 'c_dsk', 'c_wg', 'c_bg')]
    return pl.pallas_call(
        functools.partial(_c_sample_kernel, nt=nt), grid=(1,),
        in_specs=[_const_spec(a, 1) for a in args],
        out_specs=[pl.BlockSpec((nt, nb, BR), lambda c: (0, 0, 0)), pl.BlockSpec((nb, C_NS), lambda c: (0, 0)),
                   pl.BlockSpec((nb, C_NS), lambda c: (0, 0))],
        out_shape=[jax.ShapeDtypeStruct((nt, nb, BR), BF16), jax.ShapeDtypeStruct((nb, C_NS), F32),
                   jax.ShapeDtypeStruct((nb, C_NS), F32)],
        compiler_params=_cparams("arbitrary"),
    )(*args)


def _d_kernel(r_ref, k_ref, v_ref, l_ref, sh0_ref, wkv0_ref, mu_ref, w0_ref, a0_ref, kk_ref, ka_ref, rk_ref,
              gnw_ref, gnb_ref, wl_ref, al_ref, gl_ref, blk_ref, y_ref, sh_ref, wkv_ref, x_scr, ht_scr, *, t, tp,
              pa, pi, ps):
    c = pl.program_id(1)

    @pl.when(c == 0)
    def _():
        x_scr[...] = jnp.zeros_like(x_scr)
        x_scr[7:8, :] = sh0_ref[0, 0]
        ht_scr[...] = wkv0_ref[0, 0]

    x_scr[8:8 + t, 0:BR] = r_ref[0]
    x_scr[8:8 + t, BR:2 * BR] = k_ref[0]
    x_scr[8:8 + t, 2 * BR:3 * BR] = v_ref[0]
    x_scr[8:8 + t, 3 * BR:] = l_ref[0]
    cur = x_scr[8:8 + tp, :]
    xm = cur + (x_scr[7:7 + tp, :] - cur) * mu_ref[...]
    last = x_scr[7 + t:8 + t, :]
    x_scr[7:8, :] = last
    sh_ref[0] = last
    r = xm[:, :BR]
    k = xm[:, BR:2 * BR]
    v = xm[:, 2 * BR:3 * BR]
    xl = xm[:, 3 * BR:]
    lw = -jnp.exp(-_softplus(-(w0_ref[...] + _bmm(jnp.tanh(xl), wl_ref[...]))) - 0.5)
    a = _sigmoid(a0_ref[...] + _bmm(xl, al_ref[...]))
    g = _bmm(_sigmoid(xl), gl_ref[...])
    blk = blk_ref[...]
    kkr = k * kk_ref[...]
    kk = kkr / jnp.maximum(jnp.sqrt(_sel_mm(kkr * kkr, blk, 3)), 1e-12)
    k2 = k * (1.0 + (a - 1.0) * ka_ref[...])
    if t != tp:
        live = (_iota((tp, 1), 0) < t).astype(F32)
        r, k2, v, kk, lw = r * live, k2 * live, v * live, kk * live, lw * live
    bv = kk * a
    row = _iota((tp, tp), 0)
    col = _iota((tp, tp), 1)
    incl = row >= col
    strict = row > col
    eye = (row == col).astype(F32)
    cum = _pdot(incl.astype(F32), lw, 'nn', 6)
    cend = cum[tp - 1:tp, :]
    e_in = jnp.exp(cum)
    e_neg = jnp.exp(-cum)
    e_end = jnp.exp(cend - cum)
    rd = r * e_in
    kkd = kk * jnp.exp(cum - lw)
    ks = k2 * e_neg
    bs = bv * e_neg
    kse = k2 * e_end
    bse = bv * e_end
    dend = jnp.exp(cend)
    ys = []
    for h in range(NHEAD):
        sl = slice(h * HEAD, (h + 1) * HEAD)
        akb = jnp.where(strict, _pdot(kkd[:, sl], bs[:, sl], 'nt', pa), 0.0)
        akk = jnp.where(strict, _pdot(kkd[:, sl], ks[:, sl], 'nt', pa), 0.0)
        arb = jnp.where(incl, _pdot(rd[:, sl], bs[:, sl], 'nt', pa), 0.0)
        ark = jnp.where(incl, _pdot(rd[:, sl], ks[:, sl], 'nt', pa), 0.0)
        inv = eye - akb
        pw = _pdot(akb, akb, 'nn', pi)
        n = 2
        while n < tp:
            inv = inv + _pdot(inv, pw, 'nn', pi)
            n *= 2
            if n < tp:
                pw = _pdot(pw, pw, 'nn', pi)
        w1 = _pdot(inv, kkd[:, sl], 'nn', pi)
        w2 = _pdot(inv, _pdot(akk, v[:, sl], 'nn', pa), 'nn', pi)
        ht = ht_scr[h]
        u = _pdot(w1, ht, 'nt', ps) + w2
        ys.append(_pdot(rd[:, sl], ht, 'nt', ps) + _pdot(ark, v[:, sl], 'nn', pa) - _pdot(arb, u, 'nn', pa))
        ht_scr[h] = (ht * dend[:, sl] + _pdot(v[:, sl], kse[:, sl], 'tn', ps)
                     - _pdot(u, bse[:, sl], 'tn', ps))
    y = jnp.concatenate(ys, axis=-1)
    mean = _sel_mm(y, blk, 3) * (1.0 / HEAD)
    d = y - mean
    var = _sel_mm(d * d, blk, 3) * (1.0 / HEAD)
    yn = d * lax.rsqrt(var + RWKV_GN_EPS) * gnw_ref[...] + gnb_ref[...]
    bonus = _sel_mm(r * k2 * rk_ref[...], blk, 3)
    y_ref[0] = (((yn + bonus * v) * g)[:t]).astype(y_ref.dtype)
    wkv_ref[0] = ht_scr[...]


def _mixer_d(proj3, shift0, wkv0, layer, prm, *, t):
    nb, l, _ = proj3.shape
    tp = max(t, 8)
    consts = [prm[k] for k in ('d_mu', 'd_w0', 'd_a0', 'd_kk', 'd_ka', 'd_rk', 'd_gnw', 'd_gnb',
                               'd_wl', 'd_al', 'd_gl', 'd_blk')]
    return pl.pallas_call(
        functools.partial(_d_kernel, t=t, tp=tp, pa=D_PASSES[0], pi=D_PASSES[1], ps=D_PASSES[2]),
        grid=(nb, l // t),
        in_specs=[pl.BlockSpec((1, t, BR), lambda b, c: (b, c, P_R // BR)),
                  pl.BlockSpec((1, t, BR), lambda b, c: (b, c, P_DK // BR)),
                  pl.BlockSpec((1, t, BR), lambda b, c: (b, c, P_DV // BR)),
                  pl.BlockSpec((1, t, D_LORA), lambda b, c: (b, c, P_DL // D_LORA)),
                  pl.BlockSpec((1, 1, 1, D_COLS), lambda b, c: (layer, b, 0, 0)),
                  pl.BlockSpec((1, 1, NHEAD, HEAD, HEAD), lambda b, c: (layer, b, 0, 0, 0))]
        + [_const_spec(a, 2) for a in consts],
        out_specs=[pl.BlockSpec((1, t, BR), lambda b, c: (b, c, 0)),
                   pl.BlockSpec((1, 1, D_COLS), lambda b, c: (b, 0, 0)),
                   pl.BlockSpec((1, NHEAD, HEAD, HEAD), lambda b, c: (b, 0, 0, 0))],
        out_shape=[jax.ShapeDtypeStruct((nb, l, BR), BF16),
                   jax.ShapeDtypeStruct((nb, 1, D_COLS), F32),
                   jax.ShapeDtypeStruct((nb, NHEAD, HEAD, HEAD), F32)],
        scratch_shapes=[pltpu.VMEM((tp + 8, D_COLS), F32), pltpu.VMEM((NHEAD, HEAD, HEAD), F32)],
        compiler_params=_cparams("parallel", "arbitrary"),
    )(proj3, proj3, proj3, proj3, shift0, wkv0, *consts)


_IN_SPLITS = (BR, BR, BR, BR, BR, B_BC, B_BC, NHEAD, BR, BR, D_LW, BR, BR, D_LA, D_LG, N_BRANCH * D_MODEL)
_IN_NAMES = ('q', 'k', 'v', 'z', 'x', 'bm', 'cm', 'dt', 'u', 'r', 'wl', 'dk', 'dv', 'al', 'gl', 'gate')
_PACK_ORDER = ('q', 'k', 'v', 'z', 'u', 'x', 'r', 'dk', 'dv', 'gl', 'wl', 'al', 'bm', 'cm', 'gate')
_D_SPLITS = (BR, D_LW, BR, BR, D_LA, D_LG)
_D_PERM = np.concatenate([np.arange(o, o + n) for o, n in (
    (0, BR), (BR + D_LW, BR), (2 * BR + D_LW, BR), (3 * BR + D_LW + D_LA, D_LG), (BR, D_LW), (3 * BR + D_LW, D_LA))])
_D_INV = np.argsort(_D_PERM)


def _head_block(width, group):
    idx = np.arange(width) // group
    return jnp.asarray((idx[:, None] == idx[None, :]).astype(np.float32))


def _layer_params(i, P):
    prm = {}
    offs = np.cumsum((0,) + _IN_SPLITS)
    w = P['w_in'][i]
    cols = {n: w[:, offs[j]:offs[j + 1]] for j, n in enumerate(_IN_NAMES)}
    prm['w_in'] = jnp.concatenate([cols[n] for n in _PACK_ORDER], axis=1).astype(BF16)
    prm['w_dt'] = jnp.pad(cols['dt'], ((0, 0), (0, 128 - NHEAD))).astype(BF16)
    prm['norm1'] = P['norm1'][i][None]
    prm['norm2'] = P['norm2'][i][None]
    prm['w_branch'] = P['w_branch'][i].astype(BF16)
    prm['w_out'] = P['w_out'][i].astype(BF16)
    prm['w_ff1'] = P['w_ff1'][i].astype(BF16)
    prm['w_ff2'] = P['w_ff2'][i].astype(BF16)
    prm['a_qg'] = jnp.tile(P['a_q_gain'][i], NHEAD)[None]
    prm['a_kg'] = jnp.tile(P['a_k_gain'][i], NHEAD)[None]
    prm['blk64'] = _head_block(BR, HEAD)
    pad12 = lambda v: jnp.pad(v, (0, 128 - NHEAD))[None]
    prm['b_cw'] = P['b_conv_w'][i]
    prm['b_cb'] = P['b_conv_b'][i][None]
    prm['b_dtb'] = pad12(P['b_dt_bias'][i])
    prm['b_a'] = pad12(-jnp.exp(P['b_a_log'][i]))
    prm['b_dsk'] = pad12(P['b_d'][i])
    prm['b_nw'] = P['b_norm'][i][None]
    prm['b_sel'] = jnp.asarray(np.eye(16, 128, dtype=np.float32))
    prm['b_gblk'] = _head_block(BR, BR // B_GROUPS) * (B_GROUPS / BR)
    a_re, a_im = P['c_a_re'][i], P['c_a_im'][i]
    step = jnp.exp(P['c_log_step'][i])[:, None]
    mag = jnp.exp(a_re * step)
    lb_re, lb_im = mag * jnp.cos(a_im * step), mag * jnp.sin(a_im * step)
    den = a_re * a_re + a_im * a_im
    f_re = ((lb_re - 1.0) * a_re + lb_im * a_im) / den
    f_im = (lb_im * a_re - (lb_re - 1.0) * a_im) / den
    b_re, b_im = P['c_b_re'][i], P['c_b_im'][i]
    bb_re = f_re[..., None] * b_re - f_im[..., None] * b_im
    bb_im = f_re[..., None] * b_im + f_im[..., None] * b_re
    eye_g = jnp.eye(C_NG, dtype=F32)
    bfull_re = jnp.einsum('gpc,gh->gchp', bb_re, eye_g).reshape(BR, C_NS)
    bfull_im = jnp.einsum('gpc,gh->gchp', bb_im, eye_g).reshape(BR, C_NS)
    cfull_re = jnp.einsum('gcp,gh->gphc', P['c_c_re'][i], eye_g).reshape(C_NS, BR)
    cfull_im = jnp.einsum('gcp,gh->gphc', P['c_c_im'][i], eye_g).reshape(C_NS, BR)
    prm['c_lbr'] = lb_re.reshape(1, C_NS)
    prm['c_lbi'] = lb_im.reshape(1, C_NS)
    prm['c_lb8'] = jnp.concatenate([lb_re.reshape(C_Q, C_LN), lb_im.reshape(C_Q, C_LN)], axis=1)
    prm['c_bf'] = jnp.concatenate([bfull_re, bfull_im], axis=1).astype(BF16)
    prm['c_cf'] = jnp.concatenate([cfull_re, -cfull_im], axis=0).astype(BF16)
    bq, cq = [], []
    for qq in range(C_Q):
        ws, s0 = _c_window(qq), qq * C_LN
        bq.append(jnp.concatenate([bfull_re[ws:ws + C_WIN, s0:s0 + C_LN], bfull_im[ws:ws + C_WIN, s0:s0 + C_LN]], 1))
        cq.append(jnp.concatenate([cfull_re[s0:s0 + C_LN, ws:ws + C_WIN], -cfull_im[s0:s0 + C_LN, ws:ws + C_WIN]], 0))
    prm['c_bq'] = jnp.stack(bq).astype(BF16)
    prm['c_cq'] = jnp.stack(cq).astype(BF16)
    prm['c_dsk'] = P['c_d'][i][None]
    prm['c_wg'] = P['c_w_glu'][i].astype(BF16)
    prm['c_bg'] = P['c_b_glu'][i][None]
    prm['d_mu'] = P['d_mu'][i][_D_PERM][None]
    for k_, n_ in (('d_w0', 'd_w0'), ('d_a0', 'd_a0'), ('d_kk', 'd_k_k'), ('d_ka', 'd_k_a'),
                   ('d_gnw', 'd_gn_w'), ('d_gnb', 'd_gn_b')):
        prm[k_] = P[n_][i][None]
    prm['d_rk'] = P['d_r_k'][i].reshape(1, BR)
    zl = lambda r: jnp.zeros((r, BR), F32)
    prm['d_gl'] = jnp.concatenate([P['d_g_lora'][i], zl(D_LW + D_LA)], 0).astype(BF16)
    prm['d_wl'] = jnp.concatenate([zl(D_LG), P['d_w_lora'][i], zl(D_LA)], 0).astype(BF16)
    prm['d_al'] = jnp.concatenate([zl(D_LG + D_LW), P['d_a_lora'][i]], 0).astype(BF16)
    prm['d_blk'] = prm['blk64'].astype(BF16)
    return prm


def _layer(x, states, layer, prm, tabs, *, nb, prompt):
    m = x.shape[0]
    l = m // nb
    tm = min(m, 512)
    proj, dt = _norm_mm(x, prm['norm1'], prm['w_in'], prm['w_dt'], tm=tm, tn=1024)
    proj3 = proj.reshape(nb, l, P_COLS)
    dt3 = dt.reshape(nb, l, 128)
    qn, kv = _a_prep(proj, prm['a_qg'], prm['a_kg'], prm['d_blk'], tm=tm)
    if prompt:
        oa = _a_attn_prompt(qn, kv, tabs['a_bias'])
        kv_new = kv[m - min(A_WIN, m):].reshape(nb, min(A_WIN, m), 2, NHEAD, HEAD)
    else:
        oa = _a_attn_sample(qn, kv, states['kv'], layer, tabs['a_sample']).reshape(m, BR)
        kv_new = kv.reshape(nb, l, 2, NHEAD, HEAD)
    ob, conv_new, ssm_new = _mixer_b(proj3, dt3, states['conv'], states['ssm'], layer, prm, q=min(l, 128))
    if prompt:
        oc, hf = _mixer_c_prompt(proj, states['s5'], prm, t=min(l, 256))
        s5_new = jnp.stack([hf[:, :C_LN].reshape(C_NG, C_STATE), hf[:, C_LN:].reshape(C_NG, C_STATE)], -1)[None]
    else:
        u_t = proj3[:, :, P_U:P_U + BR].transpose(1, 0, 2)
        s0 = states['s5'][layer]
        oc_t, hr, hi = _mixer_c_sample(u_t, s0[..., 0].reshape(nb, C_NS), s0[..., 1].reshape(nb, C_NS), prm)
        oc = oc_t.transpose(1, 0, 2).reshape(m, BR)
        s5_new = jnp.stack([hr.reshape(nb, C_NG, C_STATE), hi.reshape(nb, C_NG, C_STATE)], -1)
    od, sh_new, wkv_new = _mixer_d(proj3, states['shift'], states['wkv'], layer, prm, t=min(l, 64))
    shift_new = sh_new.reshape(nb, D_COLS)[:, _D_INV]
    merged = _merge([oa, ob.reshape(m, BR), oc, od.reshape(m, BR)], prm['w_branch'], proj, tm=tm, tn=512)
    x = _mm_res(merged, prm['w_out'], x, tm=tm, tn=1024, tk=D_MODEL)
    hid = _norm_mm(x, prm['norm2'], prm['w_ff1'], tm=tm, tn=1024, relu2=True, out_dtype=BF16)
    x = _mm_res(hid, prm['w_ff2'], x, tm=tm, tn=1024, tk=2048)
    return x, (kv_new, conv_new, ssm_new, s5_new, shift_new, wkv_new)


def kernel(x_prompt, x_sample, cache_kv_a, state_conv, state_ssm, state_s5, state_shift, state_wkv,
           norm1, w_in, a_q_gain, a_k_gain,
           b_conv_w, b_conv_b, b_dt_bias, b_a_log, b_d, b_norm,
           c_a_re, c_a_im, c_log_step, c_b_re, c_b_im, c_c_re, c_c_im, c_d, c_w_glu, c_b_glu,
           d_mu, d_w0, d_w_lora, d_a0, d_a_lora, d_g_lora, d_k_k, d_k_a, d_r_k, d_gn_w, d_gn_b,
           w_branch, w_out, norm2, w_ff1, w_ff2):
    P = dict(norm1=norm1, w_in=w_in, a_q_gain=a_q_gain, a_k_gain=a_k_gain,
             b_conv_w=b_conv_w, b_conv_b=b_conv_b, b_dt_bias=b_dt_bias, b_a_log=b_a_log,
             b_d=b_d, b_norm=b_norm,
             c_a_re=c_a_re, c_a_im=c_a_im, c_log_step=c_log_step, c_b_re=c_b_re, c_b_im=c_b_im,
             c_c_re=c_c_re, c_c_im=c_c_im, c_d=c_d, c_w_glu=c_w_glu, c_b_glu=c_b_glu,
             d_mu=d_mu, d_w0=d_w0, d_w_lora=d_w_lora, d_a0=d_a0, d_a_lora=d_a_lora,
             d_g_lora=d_g_lora, d_k_k=d_k_k, d_k_a=d_k_a, d_r_k=d_r_k, d_gn_w=d_gn_w, d_gn_b=d_gn_b,
             w_branch=w_branch, w_out=w_out, norm2=norm2, w_ff1=w_ff1, w_ff2=w_ff2)
    depth = w_in.shape[0]
    bp, lp, _ = x_prompt.shape
    bs, ls, _ = x_sample.shape
    assert bp == 1 and ls == A_T
    tabs = {'a_bias': _a_band_bias(), 'a_sample': _a_sample_tables(cache_kv_a.shape[2])}
    zero_states = {'conv': jnp.zeros((1, bp, B_CONV - 1, B_CONV_DIM), F32),
                   'ssm': jnp.zeros((1, bp, NHEAD, HEAD, B_STATE), F32),
                   's5': jnp.zeros((C_Q, 2 * C_LN), F32),
                   'shift': jnp.zeros((1, bp, 1, D_COLS), F32),
                   'wkv': jnp.zeros((1, bp, NHEAD, HEAD, HEAD), F32)}
    samp_states = {'kv': jnp.transpose(cache_kv_a, (0, 1, 3, 4, 5, 2)), 'conv': state_conv, 'ssm': state_ssm, 's5': state_s5,
                   'shift': state_shift[:, :, _D_PERM][:, :, None, :], 'wkv': state_wkv}
    yp = x_prompt.reshape(bp * lp, D_MODEL)
    ys = x_sample.reshape(bs * ls, D_MODEL)
    new_p, new_s = [], []
    for i in range(depth):
        prm = _layer_params(i, P)
        yp, st = _layer(yp, zero_states, 0, prm, tabs, nb=bp, prompt=True)
        new_p.append(st)
        ys, st = _layer(ys, samp_states, i, prm, tabs, nb=bs, prompt=False)
        new_s.append(st)
    outs_p = [jnp.stack(z) for z in zip(*new_p)]
    outs_s = [jnp.stack(z) for z in zip(*new_s)]
    res = [yp.reshape(bp, lp, D_MODEL), ys.reshape(bs, ls, D_MODEL)]
    for a, b in zip(outs_p, outs_s):
        res += [a, b]
    return tuple(res)
```

```python
import functools

import jax
import jax.numpy as jnp
import numpy as np
from jax import lax
from jax.experimental import pallas as pl
from jax.experimental.pallas import tpu as pltpu

F32 = jnp.float32
BF16 = jnp.bfloat16
HI = lax.Precision.HIGHEST

D_MODEL = 2048
BR = 768
N_BRANCH = 4
HEAD = 64
NHEAD = BR // HEAD
A_WIN = 2048
A_BLK = 128
A_NKB = A_WIN // A_BLK + 1
A_PATTERNS = ((128, 1), (512, 4), (2048, 16))
B_GROUPS = 4
B_STATE = 128
B_CONV = 4
B_BC = B_GROUPS * B_STATE
B_CONV_DIM = BR + 2 * B_BC
C_GROUP = 16
C_NG = BR // C_GROUP
C_STATE = 64
C_NS = C_NG * C_STATE
C_Q = 8
C_LN = C_NS // C_Q
C_WIN = 256
D_LW, D_LA, D_LG = 64, 64, 128
D_LORA = D_LW + D_LA + D_LG
D_COLS = 3 * BR + D_LORA
D_FF = 4 * D_MODEL
NORM_EPS = 1e-6
RWKV_GN_EPS = 64e-5
NEG = -1e30

P_Q, P_K, P_V, P_Z, P_U, P_X, P_R, P_DK, P_DV = (BR * n for n in range(9))
P_DL = 9 * BR
P_BM = P_DL + D_LORA
P_CM = P_BM + B_BC
P_GATE = P_CM + B_BC
P_COLS = P_GATE + N_BRANCH * D_MODEL
VMEM_LIMIT = 56 * 1024 * 1024
D_PASSES = (1, 1, 3)


def _cparams(*sem):
    return pltpu.CompilerParams(dimension_semantics=sem, vmem_limit_bytes=VMEM_LIMIT)


def _mm(a, b, prec=None):
    return lax.dot_general(a, b, (((1,), (0,)), ((), ())), precision=prec, preferred_element_type=F32)


def _mm_nt(a, b, prec=None):
    return lax.dot_general(a, b, (((1,), (1,)), ((), ())), precision=prec, preferred_element_type=F32)


def _mm_tn(a, b, prec=None):
    return lax.dot_general(a, b, (((0,), (0,)), ((), ())), precision=prec, preferred_element_type=F32)


_DIMS = {'nn': (((1,), (0,)), ((), ())), 'nt': (((1,), (1,)), ((), ())), 'tn': (((0,), (0,)), ((), ()))}


def _split(x):
    hi = x.astype(BF16)
    return hi, (x - hi.astype(F32)).astype(BF16)


def _pdot(a, b, kind, passes):
    def dg(x, y, prec=None):
        return lax.dot_general(x, y, _DIMS[kind], precision=prec, preferred_element_type=F32)

    if passes == 6:
        return dg(a, b, HI)
    if passes == 1:
        return dg(a.astype(BF16), b.astype(BF16))
    ah, al = _split(a)
    bh, bl = _split(b)
    return dg(ah, bh) + (dg(ah, bl) + dg(al, bh))


def _sel_mm(x, sel_bf16, terms):
    acc = None
    for _ in range(terms):
        hi = x.astype(BF16)
        t = _mm(hi, sel_bf16)
        acc = t if acc is None else acc + t
        x = x - hi.astype(F32)
    return acc


def _bmm(a, b):
    return _mm(a.astype(BF16), b.astype(BF16))


def _bmm_nt(a, b):
    return _mm_nt(a.astype(BF16), b.astype(BF16))


def _bmm_tn(a, b):
    return _mm_tn(a.astype(BF16), b.astype(BF16))


def _iota(shape, dim):
    return lax.broadcasted_iota(jnp.int32, shape, dim)


def _sigmoid(x):
    return 1.0 / (1.0 + jnp.exp(-x))


def _softplus(x):
    return jnp.maximum(x, 0.0) + jnp.log(1.0 + jnp.exp(-jnp.abs(x)))


def _silu(x):
    return x * _sigmoid(x)


def _const_spec(a, ngrid):
    nd = a.ndim
    return pl.BlockSpec(a.shape, lambda *_: (0,) * nd)


def _norm_mm_kernel(x_ref, g_ref, w_ref, *rest, relu2, with_dt):
    if with_dt:
        wdt_ref, o_ref, odt_ref, h_scr = rest
    else:
        o_ref, h_scr = rest
    mm = _mm_nt if with_dt else _mm

    @pl.when(pl.program_id(1) == 0)
    def _():
        x = x_ref[...]
        ms = jnp.mean(x * x, axis=-1, keepdims=True)
        h = (x * lax.rsqrt(ms + NORM_EPS) * g_ref[...]).astype(BF16)
        h_scr[...] = h
        if with_dt:
            odt_ref[...] = mm(h, wdt_ref[...])

    y = mm(h_scr[...], w_ref[...])
    if relu2:
        y = jnp.square(jnp.maximum(y, 0.0))
    o_ref[...] = y.astype(o_ref.dtype)


def _norm_mm(x, g, w, wdt=None, *, tm, tn, relu2=False, out_dtype=F32):
    m, k = x.shape
    with_dt = wdt is not None
    n = w.shape[0] if with_dt else w.shape[1]
    in_specs = [pl.BlockSpec((tm, k), lambda i, j: (i, 0)),
                pl.BlockSpec((1, k), lambda i, j: (0, 0)),
                pl.BlockSpec((tn, k), lambda i, j: (j, 0)) if with_dt else pl.BlockSpec((k, tn), lambda i, j: (0, j))]
    out_specs = pl.BlockSpec((tm, tn), lambda i, j: (i, j))
    out_shape = jax.ShapeDtypeStruct((m, n), out_dtype)
    args = [x, g, w]
    if with_dt:
        in_specs.append(pl.BlockSpec((128, k), lambda i, j: (0, 0)))
        out_specs = [out_specs, pl.BlockSpec((tm, 128), lambda i, j: (i, 0))]
        out_shape = [out_shape, jax.ShapeDtypeStruct((m, 128), F32)]
        args.append(wdt)
    return pl.pallas_call(
        functools.partial(_norm_mm_kernel, relu2=relu2, with_dt=with_dt),
        grid=(m // tm, n // tn), in_specs=in_specs, out_specs=out_specs, out_shape=out_shape,
        scratch_shapes=[pltpu.VMEM((tm, k), BF16)],
        compiler_params=_cparams("parallel", "arbitrary"),
    )(*args)


def _mm_res_kernel(a_ref, w_ref, r_ref, o_ref):
    @pl.when(pl.program_id(2) == 0)
    def _():
        o_ref[...] = r_ref[...]

    o_ref[...] += _mm(a_ref[...], w_ref[...])


def _mm_res(a, w, res, *, tm, tn, tk):
    m, k = a.shape
    n = w.shape[1]
    return pl.pallas_call(
        _mm_res_kernel, grid=(m // tm, n // tn, k // tk),
        in_specs=[pl.BlockSpec((tm, tk), lambda i, j, kk: (i, kk)),
                  pl.BlockSpec((tk, tn), lambda i, j, kk: (kk, j)),
                  pl.BlockSpec((tm, tn), lambda i, j, kk: (i, j))],
        out_specs=pl.BlockSpec((tm, tn), lambda i, j, kk: (i, j)),
        out_shape=jax.ShapeDtypeStruct((m, n), F32),
        compiler_params=_cparams("parallel", "parallel", "arbitrary"),
    )(a, w, res)


def _merge_kernel(oa, ob, oc, od, wb, g0, g1, g2, g3, o_ref):
    acc = None
    for n, (o, g) in enumerate(((oa, g0), (ob, g1), (oc, g2), (od, g3))):
        t = _sigmoid(g[...]) * _mm(o[...], wb[n])
        acc = t if acc is None else acc + t
    o_ref[...] = acc.astype(o_ref.dtype)


def _merge(outs, wb, proj, *, tm, tn):
    m = proj.shape[0]
    o_spec = pl.BlockSpec((tm, BR), lambda i, j: (i, 0))

    def g_spec(n):
        return pl.BlockSpec((tm, tn), lambda i, j: (i, (P_GATE + n * D_MODEL) // tn + j))

    return pl.pallas_call(
        _merge_kernel, grid=(m // tm, D_MODEL // tn),
        in_specs=[o_spec] * 4 + [pl.BlockSpec((N_BRANCH, BR, tn), lambda i, j: (0, 0, j))]
        + [g_spec(n) for n in range(N_BRANCH)],
        out_specs=pl.BlockSpec((tm, tn), lambda i, j: (i, j)),
        out_shape=jax.ShapeDtypeStruct((m, D_MODEL), BF16),
        compiler_params=_cparams("parallel", "parallel"),
    )(*outs, wb, proj, proj, proj, proj)


def _a_prep_kernel(q_ref, k_ref, v_ref, qg_ref, kg_ref, blk_ref, qn_ref, kv_ref):
    blk = blk_ref[...]
    q = q_ref[...]
    k = k_ref[...]
    qms = _sel_mm(q * q, blk, 3) * (1.0 / HEAD)
    kms = _sel_mm(k * k, blk, 3) * (1.0 / HEAD)
    qn_ref[...] = q * lax.rsqrt(qms + NORM_EPS) * qg_ref[...] * (HEAD ** -0.5)
    kv_ref[:, :BR] = k * lax.rsqrt(kms + NORM_EPS) * kg_ref[...]
    kv_ref[:, BR:] = v_ref[...]


def _a_prep(proj, qg, kg, blk, *, tm):
    m = proj.shape[0]

    def col(c):
        return pl.BlockSpec((tm, BR), lambda i: (i, c))

    vec = pl.BlockSpec((1, BR), lambda i: (0, 0))
    return pl.pallas_call(
        _a_prep_kernel, grid=(m // tm,),
        in_specs=[col(P_Q // BR), col(P_K // BR), col(P_V // BR), vec, vec, pl.BlockSpec((BR, BR), lambda i: (0, 0))],
        out_specs=[pl.BlockSpec((tm, BR), lambda i: (i, 0)), pl.BlockSpec((tm, 2 * BR), lambda i: (i, 0))],
        out_shape=[jax.ShapeDtypeStruct((m, BR), F32), jax.ShapeDtypeStruct((m, 2 * BR), F32)],
        compiler_params=_cparams("parallel"),
    )(proj, proj, proj, qg, kg, blk)


def _a_band_bias():
    a = np.arange(A_BLK)[:, None]
    c = np.arange(2 * A_BLK)[None, :]
    band = (c >= a) & (c <= a + A_BLK)
    return jnp.asarray(np.stack([np.where(band, 0.0, NEG), np.where(band & (c >= A_BLK), 0.0, NEG)]).astype(np.float32))


def _a_attn_kernel(q_ref, kp_ref, kc_ref, vp_ref, vc_ref, bias_ref, o_ref, k_scr, v_scr, m_scr, l_scr, acc_scr,
                   *, sb_len):
    sb = pl.program_id(1)
    k_scr[0:sb_len, :] = kp_ref[...]
    k_scr[sb_len:, :] = kc_ref[...]
    v_scr[0:sb_len, :] = vp_ref[...]
    v_scr[sb_len:, :] = vc_ref[...]
    lane = _iota((A_BLK, 2 * HEAD), 1)
    left = lane < HEAD
    first_sb = sb == 0
    for pi_, (w, d) in enumerate(A_PATTERNS):
        assert w // d == A_BLK and sb_len % (A_BLK * d) == 0
        for j in range(sb_len // (A_BLK * d)):
            for r in range(d):
                q0 = r + d * A_BLK * j
                k0 = sb_len + q0 - d * A_BLK
                rows_q = pl.ds(q0, A_BLK, stride=d) if d > 1 else pl.ds(q0, A_BLK)
                rows_k = pl.ds(k0, 2 * A_BLK, stride=d) if d > 1 else pl.ds(k0, 2 * A_BLK)
                q = q_ref[rows_q, :]
                kt = k_scr[rows_k, :].astype(BF16)
                vt = v_scr[rows_k, :].astype(BF16)
                if j == 0:
                    bias = jnp.where(first_sb, bias_ref[1], bias_ref[0])
                else:
                    bias = bias_ref[0]
                ms, ls, accs = [], [], []
                for h in range(2):
                    qh = jnp.where(left if h == 0 else ~left, q, 0.0).astype(BF16)
                    s = _mm_nt(qh, kt) + bias
                    mu = jnp.max(s, axis=-1, keepdims=True)
                    p = jnp.exp(s - mu)
                    ms.append(mu)
                    ls.append(jnp.sum(p, axis=-1, keepdims=True))
                    accs.append(_mm(p.astype(BF16), vt))
                mu = jnp.where(left, ms[0], ms[1])
                lu = jnp.where(left, ls[0], ls[1])
                au = jnp.where(left, accs[0], accs[1])
                if pi_ == 0:
                    m_scr[rows_q, :] = mu
                    l_scr[rows_q, :] = lu
                    acc_scr[rows_q, :] = au
                else:
                    mo = m_scr[rows_q, :]
                    mn = jnp.maximum(mo, mu)
                    eo = jnp.exp(mo - mn)
                    eu = jnp.exp(mu - mn)
                    m_scr[rows_q, :] = mn
                    l_scr[rows_q, :] = eo * l_scr[rows_q, :] + eu * lu
                    acc_scr[rows_q, :] = eo * acc_scr[rows_q, :] + eu * au
    o_ref[...] = (acc_scr[...] / l_scr[...]).astype(o_ref.dtype)


def _a_attn_prompt(qn, kv, bias):
    l = qn.shape[0]
    npair = NHEAD // 2
    sb_len = min(l, A_WIN)
    blk = (sb_len, 2 * HEAD)
    prev = lambda hp, s: (jnp.maximum(s - 1, 0), hp)
    prev_v = lambda hp, s: (jnp.maximum(s - 1, 0), npair + hp)
    return pl.pallas_call(
        functools.partial(_a_attn_kernel, sb_len=sb_len), grid=(npair, l // sb_len),
        in_specs=[pl.BlockSpec(blk, lambda hp, s: (s, hp)),
                  pl.BlockSpec(blk, prev), pl.BlockSpec(blk, lambda hp, s: (s, hp)),
                  pl.BlockSpec(blk, prev_v), pl.BlockSpec(blk, lambda hp, s: (s, npair + hp)),
                  _const_spec(bias, 2)],
        out_specs=pl.BlockSpec(blk, lambda hp, s: (s, hp)),
        out_shape=jax.ShapeDtypeStruct((l, BR), BF16),
        scratch_shapes=[pltpu.VMEM((2 * sb_len, 2 * HEAD), F32), pltpu.VMEM((2 * sb_len, 2 * HEAD), F32)]
        + [pltpu.VMEM(blk, F32)] * 3,
        compiler_params=_cparams("parallel", "arbitrary"),
    )(qn, kv, kv, kv, kv, bias)


A_T = 4


def _a_sample_tables(past):
    row_t = np.arange(64)[:, None] // 16
    d = past + row_t - np.arange(past)[None, :]
    mc = np.zeros(d.shape, np.float32)
    for w, dil in A_PATTERNS:
        mc += ((d >= 0) & (d <= w) & (d % dil == 0)).astype(np.float32)
    tk = np.arange(8)[None, :]
    mn = np.where((tk <= row_t) & (tk < A_T), 1.0 + (len(A_PATTERNS) - 1.0) * (tk == row_t), 0.0).astype(np.float32)
    hm = np.zeros((16, BR), np.float32)
    for h in range(NHEAD):
        hm[h, h * HEAD:(h + 1) * HEAD] = 1.0
    return jnp.asarray(mc), jnp.asarray(mn), jnp.asarray(hm)


def _a_sample_kernel(q_ref, kvn_ref, c_ref, hm_ref, mc_ref, mn_ref, o_ref, new_scr, *, past):
    hm = hm_ref[...]
    q = q_ref[0]
    qexp = jnp.concatenate([jnp.broadcast_to(q[t:t + 1, :], (16, BR)) * hm for t in range(A_T)], axis=0)
    qexp = qexp.astype(BF16)
    new_scr[...] = jnp.zeros_like(new_scr)
    new_scr[0:A_T, :] = kvn_ref[0]
    kt = c_ref[0, 0, 0].reshape(BR, past).astype(BF16)
    vt = c_ref[0, 0, 1].reshape(BR, past).astype(BF16)
    mc = mc_ref[...]
    mn = mn_ref[...]
    s = jnp.where(mc > 0.0, _mm(qexp, kt), NEG)
    sn = jnp.where(mn > 0.0, _mm_nt(qexp, new_scr[:, :BR].astype(BF16)), NEG)
    m = jnp.maximum(jnp.max(s, axis=-1, keepdims=True), jnp.max(sn, axis=-1, keepdims=True))
    p = mc * jnp.exp(s - m)
    pn = mn * jnp.exp(sn - m)
    den = jnp.sum(p, axis=-1, keepdims=True) + jnp.sum(pn, axis=-1, keepdims=True)
    r = (_mm_nt(p.astype(BF16), vt) + _mm(pn.astype(BF16), new_scr[:, BR:].astype(BF16))) / den
    for t in range(A_T):
        o_ref[0, t:t + 1, :] = jnp.sum(r[16 * t:16 * t + 16, :] * hm, axis=0, keepdims=True).astype(o_ref.dtype)


def _a_attn_sample(qn, kv_new, cache_t, layer, tables):
    nb, past = cache_t.shape[1], cache_t.shape[-1]
    mc, mn, hm = tables
    return pl.pallas_call(
        functools.partial(_a_sample_kernel, past=past), grid=(nb,),
        in_specs=[pl.BlockSpec((1, A_T, BR), lambda b: (b, 0, 0)),
                  pl.BlockSpec((1, A_T, 2 * BR), lambda b: (b, 0, 0)),
                  pl.BlockSpec((1, 1, 2, NHEAD, HEAD, past), lambda b: (layer, b, 0, 0, 0, 0)),
                  _const_spec(hm, 1), _const_spec(mc, 1), _const_spec(mn, 1)],
        out_specs=pl.BlockSpec((1, A_T, BR), lambda b: (b, 0, 0)),
        out_shape=jax.ShapeDtypeStruct((nb, A_T, BR), BF16),
        scratch_shapes=[pltpu.VMEM((8, 2 * BR), F32)],
        compiler_params=_cparams("parallel"),
    )(qn.reshape(nb, A_T, BR), kv_new.reshape(nb, A_T, 2 * BR), cache_t, hm, mc, mn)


def _b_kernel(z_ref, x_ref, bm_ref, cm_ref, dt_ref, conv0_ref, ssm0_ref, cw_ref, cb_ref, dtb_ref, a_ref, dsk_ref,
              nw_ref, sel_ref, gblk_ref, y_ref, conv_ref, ssm_ref, xp_scr, s_scr, pad_scr, *, q, qp):
    c = pl.program_id(1)

    @pl.when(c == 0)
    def _():
        xp_scr[...] = jnp.zeros_like(xp_scr)
        xp_scr[5:8, :] = conv0_ref[0, 0]
        s_scr[...] = ssm0_ref[0, 0]

    xp_scr[8:8 + q, 0:BR] = x_ref[0]
    xp_scr[8:8 + q, BR:BR + B_BC] = bm_ref[0]
    xp_scr[8:8 + q, BR + B_BC:] = cm_ref[0]
    cw = cw_ref[...]
    conv = (cb_ref[...] + cw[3:4] * xp_scr[8:8 + qp, :] + cw[2:3] * xp_scr[7:7 + qp, :]
            + cw[1:2] * xp_scr[6:6 + qp, :] + cw[0:1] * xp_scr[5:5 + qp, :])
    tail = xp_scr[5 + q:8 + q, :]
    xp_scr[5:8, :] = tail
    conv_ref[0] = tail
    u = _silu(conv)
    xs = u[:, :BR]
    dt = _softplus(dt_ref[0] + dtb_ref[...])
    z = z_ref[0]
    if q != qp:
        pad_scr[...] = jnp.zeros_like(pad_scr)
        pad_scr[0:q, :BR] = z
        pad_scr[0:q, BR:] = dt
        z = pad_scr[:, :BR]
        dt = pad_scr[:, BR:]
    adt = dt * a_ref[...]
    trib = _iota((qp, qp), 0) >= _iota((qp, qp), 1)
    acum = _mm(trib.astype(F32), adt, HI)
    acum_t = _mm_nt(sel_ref[...], acum, HI)
    alast = acum[qp - 1:qp, :]
    dend = jnp.exp(alast - acum)
    eacum = jnp.exp(acum)
    elast = jnp.exp(alast)
    ys = []
    hpg = NHEAD // B_GROUPS
    for g in range(B_GROUPS):
        bm = u[:, BR + g * B_STATE:BR + (g + 1) * B_STATE]
        cm = u[:, BR + B_BC + g * B_STATE:BR + B_BC + (g + 1) * B_STATE]
        gmat = _bmm_nt(cm, bm)
        for h in range(g * hpg, (g + 1) * hpg):
            xh = xs[:, h * HEAD:(h + 1) * HEAD]
            xdt = xh * dt[:, h:h + 1]
            lmat = jnp.where(trib, jnp.exp(acum[:, h:h + 1] - acum_t[h:h + 1, :]), 0.0)
            s_h = s_scr[h]
            ys.append(_bmm(gmat * lmat, xdt) + eacum[:, h:h + 1] * _bmm_nt(cm, s_h) + xh * dsk_ref[:, h:h + 1])
            s_scr[h] = elast[:, h:h + 1] * s_h + _bmm_tn(xdt * dend[:, h:h + 1], bm)
    y = jnp.concatenate(ys, axis=-1) * _silu(z)
    ms = _mm(y * y, gblk_ref[...], HI)
    y = y * lax.rsqrt(ms + NORM_EPS) * nw_ref[...]
    y_ref[0] = y[:q].astype(y_ref.dtype)
    ssm_ref[0] = s_scr[...]


def _mixer_b(proj3, dt3, conv0, ssm0, layer, prm, *, q):
    nb, l, _ = proj3.shape
    qp = max(q, 8)
    consts = [prm[k] for k in ('b_cw', 'b_cb', 'b_dtb', 'b_a', 'b_dsk', 'b_nw', 'b_sel', 'b_gblk')]
    return pl.pallas_call(
        functools.partial(_b_kernel, q=q, qp=qp), grid=(nb, l // q),
        in_specs=[pl.BlockSpec((1, q, BR), lambda b, c: (b, c, P_Z // BR)),
                  pl.BlockSpec((1, q, BR), lambda b, c: (b, c, P_X // BR)),
                  pl.BlockSpec((1, q, B_BC), lambda b, c: (b, c, P_BM // B_BC)),
                  pl.BlockSpec((1, q, B_BC), lambda b, c: (b, c, P_CM // B_BC)),
                  pl.BlockSpec((1, q, 128), lambda b, c: (b, c, 0)),
                  pl.BlockSpec((1, 1, B_CONV - 1, B_CONV_DIM), lambda b, c: (layer, b, 0, 0)),
                  pl.BlockSpec((1, 1, NHEAD, HEAD, B_STATE), lambda b, c: (layer, b, 0, 0, 0))]
        + [_const_spec(a, 2) for a in consts],
        out_specs=[pl.BlockSpec((1, q, BR), lambda b, c: (b, c, 0)),
                   pl.BlockSpec((1, B_CONV - 1, B_CONV_DIM), lambda b, c: (b, 0, 0)),
                   pl.BlockSpec((1, NHEAD, HEAD, B_STATE), lambda b, c: (b, 0, 0, 0))],
        out_shape=[jax.ShapeDtypeStruct((nb, l, BR), BF16),
                   jax.ShapeDtypeStruct((nb, B_CONV - 1, B_CONV_DIM), F32),
                   jax.ShapeDtypeStruct((nb, NHEAD, HEAD, B_STATE), F32)],
        scratch_shapes=[pltpu.VMEM((qp + 8, B_CONV_DIM), F32), pltpu.VMEM((NHEAD, HEAD, B_STATE), F32),
                        pltpu.VMEM((qp, BR + 128), F32)],
        compiler_params=_cparams("parallel", "arbitrary"),
    )(proj3, proj3, proj3, proj3, dt3, conv0, ssm0, *consts)


def _gelu_tanh(y):
    return 0.5 * y * (1.0 + jnp.tanh(0.7978845608028654 * (y + 0.044715 * (y * y * y))))


def _c_window(qq):
    first = qq * C_LN // C_STATE * C_GROUP
    return min(first // 128 * 128, BR - C_WIN)


def _c_prompt_kernel(u_ref, h0_ref, lb_ref, bq_ref, cq_ref, dsk_ref, wg_ref, bg_ref, o_ref, hf_ref,
                     bu_scr, hs_scr, h_scr, *, t):
    @pl.when(pl.program_id(0) == 0)
    def _():
        h_scr[...] = h0_ref[...]

    u = u_ref[...]
    ub = u.astype(BF16)
    ntile = 2 * C_LN // 128
    for qq in range(C_Q):
        ws = _c_window(qq)
        bu = _mm(ub[:, ws:ws + C_WIN], bq_ref[qq])
        for j in range(ntile):
            bu_scr[j, pl.ds(qq, t, stride=C_Q), :] = bu[:, j * 128:(j + 1) * 128]
    lb = [lb_ref[:, j * 128:(j + 1) * 128] for j in range(ntile)]
    half = ntile // 2

    def step(i, h):
        r0 = pl.multiple_of(i * C_Q, C_Q)
        new = []
        for j in range(half):
            new.append(lb[j] * h[j] - lb[half + j] * h[half + j] + bu_scr[j, pl.ds(r0, C_Q), :])
        for j in range(half):
            new.append(lb[j] * h[half + j] + lb[half + j] * h[j] + bu_scr[half + j, pl.ds(r0, C_Q), :])
        for j in range(ntile):
            hs_scr[j, pl.ds(r0, C_Q), :] = new[j]
        return tuple(new)

    h = lax.fori_loop(0, t, step, tuple(h_scr[:, j * 128:(j + 1) * 128] for j in range(ntile)), unroll=8)
    for j in range(ntile):
        h_scr[:, j * 128:(j + 1) * 128] = h[j]
    hf_ref[...] = h_scr[...]
    ytile = [None] * (BR // 128)
    for qq in range(C_Q):
        ws = _c_window(qq) // 128
        hq = jnp.concatenate([hs_scr[j, pl.ds(qq, t, stride=C_Q), :] for j in range(ntile)], axis=-1)
        yq = _mm(hq.astype(BF16), cq_ref[qq])
        for j in range(C_WIN // 128):
            part = yq[:, j * 128:(j + 1) * 128]
            ytile[ws + j] = part if ytile[ws + j] is None else ytile[ws + j] + part
    y = _gelu_tanh(jnp.concatenate(ytile, axis=-1) + dsk_ref[...] * u)
    o_ref[...] = (y * _sigmoid(_bmm(y, wg_ref[...]) + bg_ref[...])).astype(o_ref.dtype)


def _mixer_c_prompt(proj, h0, prm, *, t):
    l = proj.shape[0]
    consts = [prm[k] for k in ('c_lb8', 'c_bq', 'c_cq', 'c_dsk', 'c_wg', 'c_bg')]
    return pl.pallas_call(
        functools.partial(_c_prompt_kernel, t=t), grid=(l // t,),
        in_specs=[pl.BlockSpec((t, BR), lambda c: (c, P_U // BR)), _const_spec(h0, 1)]
        + [_const_spec(a, 1) for a in consts],
        out_specs=[pl.BlockSpec((t, BR), lambda c: (c, 0)), pl.BlockSpec((C_Q, 2 * C_LN), lambda c: (0, 0))],
        out_shape=[jax.ShapeDtypeStruct((l, BR), BF16), jax.ShapeDtypeStruct((C_Q, 2 * C_LN), F32)],
        scratch_shapes=[pltpu.VMEM((2 * C_LN // 128, C_Q * t, 128), F32),
                        pltpu.VMEM((2 * C_LN // 128, C_Q * t, 128), F32),
                        pltpu.VMEM((C_Q, 2 * C_LN), F32)],
        compiler_params=_cparams("arbitrary"),
    )(proj, h0, *consts)


def _c_sample_kernel(u_ref, hr0_ref, hi0_ref, lbr_ref, lbi_ref, bf_ref, cf_ref, dsk_ref, wg_ref, bg_ref,
                     o_ref, hr_ref, hi_ref, *, nt):
    hr = hr0_ref[...]
    hi = hi0_ref[...]
    lbr = lbr_ref[...]
    lbi = lbi_ref[...]
    for t in range(nt):
        u = u_ref[t]
        bu = _bmm(u, bf_ref[...])
        hr, hi = lbr * hr - lbi * hi + bu[:, :C_NS], lbr * hi + lbi * hr + bu[:, C_NS:]
        y = _bmm(hr, cf_ref[:C_NS, :]) + _bmm(hi, cf_ref[C_NS:, :]) + dsk_ref[...] * u
        y = _gelu_tanh(y)
        o_ref[t] = (y * _sigmoid(_bmm(y, wg_ref[...]) + bg_ref[...])).astype(o_ref.dtype)
    hr_ref[...] = hr
    hi_ref[...] = hi


def _mixer_c_sample(u_t, hr0, hi0, prm):
    nt, nb, _ = u_t.shape
    args = [u_t, hr0, hi0] + [prm[k] for k in ('c_lbr', 'c_lbi', 'c_bf', 'c_cf', 'c_dsk', 'c_wg', 'c_bg')]
    return pl.pallas_call(
        functools.partial(_c_sample_kernel, nt=nt), grid=(1,),
        in_specs=[_const_spec(a, 1) for a in args],
        out_specs=[pl.BlockSpec((nt, nb, BR), lambda c: (0, 0, 0)), pl.BlockSpec((nb, C_NS), lambda c: (0, 0)),
                   pl.BlockSpec((nb, C_NS), lambda c: (0, 0))],
        out_shape=[jax.ShapeDtypeStruct((nt, nb, BR), BF16), jax.ShapeDtypeStruct((nb, C_NS), F32),
                   jax.ShapeDtypeStruct((nb, C_NS), F32)],
        compiler_params=_cparams("arbitrary"),
    )(*args)


def _d_kernel(r_ref, k_ref, v_ref, l_ref, sh0_ref, wkv0_ref, mu_ref, w0_ref, a0_ref, kk_ref, ka_ref, rk_ref,
              gnw_ref, gnb_ref, wl_ref, al_ref, gl_ref, blk_ref, y_ref, sh_ref, wkv_ref, x_scr, ht_scr, *, t, tp,
              pa, pi, ps):
    c = pl.program_id(1)

    @pl.when(c == 0)
    def _():
        x_scr[...] = jnp.zeros_like(x_scr)
        x_scr[7:8, :] = sh0_ref[0, 0]
        zero = jnp.zeros((HEAD, HEAD), F32)
        for p in range(NHEAD // 2):
            ht_scr[p, 0:HEAD, :] = jnp.concatenate([wkv0_ref[0, 0, 2 * p], zero], axis=-1)
            ht_scr[p, HEAD:, :] = jnp.concatenate([zero, wkv0_ref[0, 0, 2 * p + 1]], axis=-1)

    x_scr[8:8 + t, 0:BR] = r_ref[0]
    x_scr[8:8 + t, BR:2 * BR] = k_ref[0]
    x_scr[8:8 + t, 2 * BR:3 * BR] = v_ref[0]
    x_scr[8:8 + t, 3 * BR:] = l_ref[0]
    cur = x_scr[8:8 + tp, :]
    xm = cur + (x_scr[7:7 + tp, :] - cur) * mu_ref[...]
    last = x_scr[7 + t:8 + t, :]
    x_scr[7:8, :] = last
    sh_ref[0] = last
    r = xm[:, :BR]
    k = xm[:, BR:2 * BR]
    v = xm[:, 2 * BR:3 * BR]
    xl = xm[:, 3 * BR:]
    lw = -jnp.exp(-_softplus(-(w0_ref[...] + _bmm(jnp.tanh(xl), wl_ref[...]))) - 0.5)
    a = _sigmoid(a0_ref[...] + _bmm(xl, al_ref[...]))
    g = _bmm(_sigmoid(xl), gl_ref[...])
    blk = blk_ref[...]
    kkr = k * kk_ref[...]
    kk = kkr / jnp.maximum(jnp.sqrt(_sel_mm(kkr * kkr, blk, 3)), 1e-12)
    k2 = k * (1.0 + (a - 1.0) * ka_ref[...])
    if t != tp:
        live = (_iota((tp, 1), 0) < t).astype(F32)
        r, k2, v, kk, lw = r * live, k2 * live, v * live, kk * live, lw * live
    bv = kk * a
    row = _iota((tp, tp), 0)
    col = _iota((tp, tp), 1)
    cum = _pdot((row >= col).astype(F32), lw, 'nn', 6)
    cend = cum[tp - 1:tp, :]
    e_in = jnp.exp(cum)
    e_neg = jnp.exp(-cum)
    e_end = jnp.exp(cend - cum)
    rd = r * e_in
    kkd = kk * jnp.exp(cum - lw)
    ks = k2 * e_neg
    bs = bv * e_neg
    kse = k2 * e_end
    bse = bv * e_end
    dend = jnp.exp(cend)
    tp2 = 2 * tp
    row2 = _iota((tp2, tp2), 0) % tp
    col2 = _iota((tp2, tp2), 1) % tp
    incl = row2 >= col2
    strict = row2 > col2
    eye = (_iota((tp2, tp2), 0) == _iota((tp2, tp2), 1)).astype(F32)
    left = _iota((tp, 2 * HEAD), 1) < HEAD

    def stack(x, p):
        xp = x[:, 2 * HEAD * p:2 * HEAD * (p + 1)]
        return jnp.concatenate([jnp.where(left, xp, 0.0), jnp.where(left, 0.0, xp)], axis=0)

    ys = []
    for p in range(NHEAD // 2):
        kkd_s, bs_s, ks_s, rd_s, v_s = stack(kkd, p), stack(bs, p), stack(ks, p), stack(rd, p), stack(v, p)
        akb = jnp.where(strict, _pdot(kkd_s, bs_s, 'nt', pa), 0.0)
        akk = jnp.where(strict, _pdot(kkd_s, ks_s, 'nt', pa), 0.0)
        arb = jnp.where(incl, _pdot(rd_s, bs_s, 'nt', pa), 0.0)
        ark = jnp.where(incl, _pdot(rd_s, ks_s, 'nt', pa), 0.0)
        inv = eye - akb
        pw = _pdot(akb, akb, 'nn', pi)
        n = 2
        while n < tp:
            inv = inv + _pdot(inv, pw, 'nn', pi)
            n *= 2
            if n < tp:
                pw = _pdot(pw, pw, 'nn', pi)
        w1 = _pdot(inv, kkd_s, 'nn', pi)
        w2 = _pdot(inv, _pdot(akk, v_s, 'nn', pa), 'nn', pi)
        ht = ht_scr[p]
        u = _pdot(w1, ht, 'nt', ps) + w2
        y_s = _pdot(rd_s, ht, 'nt', ps) + _pdot(ark, v_s, 'nn', pa) - _pdot(arb, u, 'nn', pa)
        ys.append(y_s[:tp] + y_s[tp:])
        ht_scr[p] = (ht * dend[:, 2 * HEAD * p:2 * HEAD * (p + 1)] + _pdot(v_s, stack(kse, p), 'tn', ps)
                     - _pdot(u, stack(bse, p), 'tn', ps))
    y = jnp.concatenate(ys, axis=-1)
    mean = _sel_mm(y, blk, 3) * (1.0 / HEAD)
    d = y - mean
    var = _sel_mm(d * d, blk, 3) * (1.0 / HEAD)
    yn = d * lax.rsqrt(var + RWKV_GN_EPS) * gnw_ref[...] + gnb_ref[...]
    bonus = _sel_mm(r * k2 * rk_ref[...], blk, 3)
    y_ref[0] = (((yn + bonus * v) * g)[:t]).astype(y_ref.dtype)

    @pl.when(c == pl.num_programs(1) - 1)
    def _():
        for p in range(NHEAD // 2):
            wkv_ref[0, 2 * p] = ht_scr[p, 0:HEAD, 0:HEAD]
            wkv_ref[0, 2 * p + 1] = ht_scr[p, HEAD:, HEAD:]


def _mixer_d(proj3, shift0, wkv0, layer, prm, *, t):
    nb, l, _ = proj3.shape
    tp = max(t, 8)
    consts = [prm[k] for k in ('d_mu', 'd_w0', 'd_a0', 'd_kk', 'd_ka', 'd_rk', 'd_gnw', 'd_gnb',
                               'd_wl', 'd_al', 'd_gl', 'd_blk')]
    return pl.pallas_call(
        functools.partial(_d_kernel, t=t, tp=tp, pa=D_PASSES[0], pi=D_PASSES[1], ps=D_PASSES[2]),
        grid=(nb, l // t),
        in_specs=[pl.BlockSpec((1, t, BR), lambda b, c: (b, c, P_R // BR)),
                  pl.BlockSpec((1, t, BR), lambda b, c: (b, c, P_DK // BR)),
                  pl.BlockSpec((1, t, BR), lambda b, c: (b, c, P_DV // BR)),
                  pl.BlockSpec((1, t, D_LORA), lambda b, c: (b, c, P_DL // D_LORA)),
                  pl.BlockSpec((1, 1, 1, D_COLS), lambda b, c: (layer, b, 0, 0)),
                  pl.BlockSpec((1, 1, NHEAD, HEAD, HEAD), lambda b, c: (layer, b, 0, 0, 0))]
        + [_const_spec(a, 2) for a in consts],
        out_specs=[pl.BlockSpec((1, t, BR), lambda b, c: (b, c, 0)),
                   pl.BlockSpec((1, 1, D_COLS), lambda b, c: (b, 0, 0)),
                   pl.BlockSpec((1, NHEAD, HEAD, HEAD), lambda b, c: (b, 0, 0, 0))],
        out_shape=[jax.ShapeDtypeStruct((nb, l, BR), BF16),
                   jax.ShapeDtypeStruct((nb, 1, D_COLS), F32),
                   jax.ShapeDtypeStruct((nb, NHEAD, HEAD, HEAD), F32)],
        scratch_shapes=[pltpu.VMEM((tp + 8, D_COLS), F32), pltpu.VMEM((NHEAD // 2, 2 * HEAD, 2 * HEAD), F32)],
        compiler_params=_cparams("parallel", "arbitrary"),
    )(proj3, proj3, proj3, proj3, shift0, wkv0, *consts)


_IN_SPLITS = (BR, BR, BR, BR, BR, B_BC, B_BC, NHEAD, BR, BR, D_LW, BR, BR, D_LA, D_LG, N_BRANCH * D_MODEL)
_IN_NAMES = ('q', 'k', 'v', 'z', 'x', 'bm', 'cm', 'dt', 'u', 'r', 'wl', 'dk', 'dv', 'al', 'gl', 'gate')
_PACK_ORDER = ('q', 'k', 'v', 'z', 'u', 'x', 'r', 'dk', 'dv', 'gl', 'wl', 'al', 'bm', 'cm', 'gate')
_D_SPLITS = (BR, D_LW, BR, BR, D_LA, D_LG)
_D_PERM = np.concatenate([np.arange(o, o + n) for o, n in (
    (0, BR), (BR + D_LW, BR), (2 * BR + D_LW, BR), (3 * BR + D_LW + D_LA, D_LG), (BR, D_LW), (3 * BR + D_LW, D_LA))])
_D_INV = np.argsort(_D_PERM)


def _head_block(width, group):
    idx = np.arange(width) // group
    return jnp.asarray((idx[:, None] == idx[None, :]).astype(np.float32))


def _layer_params(i, P):
    prm = {}
    offs = np.cumsum((0,) + _IN_SPLITS)
    wt = jnp.transpose(P['w_in'], (2, 0, 1))[:, i, :]
    cols = {n: wt[offs[j]:offs[j + 1]] for j, n in enumerate(_IN_NAMES)}
    prm['w_in'] = jnp.concatenate([cols[n] for n in _PACK_ORDER], axis=0).astype(BF16)
    prm['w_dt'] = jnp.pad(cols['dt'], ((0, 128 - NHEAD), (0, 0))).astype(BF16)
    prm['norm1'] = P['norm1'][i][None]
    prm['norm2'] = P['norm2'][i][None]
    prm['w_branch'] = P['w_branch'][i].astype(BF16)
    prm['w_out'] = P['w_out'][i].astype(BF16)
    prm['w_ff1'] = P['w_ff1'][i].astype(BF16)
    prm['w_ff2'] = P['w_ff2'][i].astype(BF16)
    prm['a_qg'] = jnp.tile(P['a_q_gain'][i], NHEAD)[None]
    prm['a_kg'] = jnp.tile(P['a_k_gain'][i], NHEAD)[None]
    prm['blk64'] = _head_block(BR, HEAD)
    pad12 = lambda v: jnp.pad(v, (0, 128 - NHEAD))[None]
    prm['b_cw'] = P['b_conv_w'][i]
    prm['b_cb'] = P['b_conv_b'][i][None]
    prm['b_dtb'] = pad12(P['b_dt_bias'][i])
    prm['b_a'] = pad12(-jnp.exp(P['b_a_log'][i]))
    prm['b_dsk'] = pad12(P['b_d'][i])
    prm['b_nw'] = P['b_norm'][i][None]
    prm['b_sel'] = jnp.asarray(np.eye(16, 128, dtype=np.float32))
    prm['b_gblk'] = _head_block(BR, BR // B_GROUPS) * (B_GROUPS / BR)
    a_re, a_im = P['c_a_re'][i], P['c_a_im'][i]
    step = jnp.exp(P['c_log_step'][i])[:, None]
    mag = jnp.exp(a_re * step)
    lb_re, lb_im = mag * jnp.cos(a_im * step), mag * jnp.sin(a_im * step)
    den = a_re * a_re + a_im * a_im
    f_re = ((lb_re - 1.0) * a_re + lb_im * a_im) / den
    f_im = (lb_im * a_re - (lb_re - 1.0) * a_im) / den
    b_re, b_im = P['c_b_re'][i], P['c_b_im'][i]
    bb_re = f_re[..., None] * b_re - f_im[..., None] * b_im
    bb_im = f_re[..., None] * b_im + f_im[..., None] * b_re
    eye_g = jnp.eye(C_NG, dtype=F32)
    bfull_re = jnp.einsum('gpc,gh->gchp', bb_re, eye_g).reshape(BR, C_NS)
    bfull_im = jnp.einsum('gpc,gh->gchp', bb_im, eye_g).reshape(BR, C_NS)
    cfull_re = jnp.einsum('gcp,gh->gphc', P['c_c_re'][i], eye_g).reshape(C_NS, BR)
    cfull_im = jnp.einsum('gcp,gh->gphc', P['c_c_im'][i], eye_g).reshape(C_NS, BR)
    prm['c_lbr'] = lb_re.reshape(1, C_NS)
    prm['c_lbi'] = lb_im.reshape(1, C_NS)
    prm['c_lb8'] = jnp.concatenate([lb_re.reshape(C_Q, C_LN), lb_im.reshape(C_Q, C_LN)], axis=1)
    prm['c_bf'] = jnp.concatenate([bfull_re, bfull_im], axis=1).astype(BF16)
    prm['c_cf'] = jnp.concatenate([cfull_re, -cfull_im], axis=0).astype(BF16)
    bq, cq = [], []
    for qq in range(C_Q):
        ws, s0 = _c_window(qq), qq * C_LN
        bq.append(jnp.concatenate([bfull_re[ws:ws + C_WIN, s0:s0 + C_LN], bfull_im[ws:ws + C_WIN, s0:s0 + C_LN]], 1))
        cq.append(jnp.concatenate([cfull_re[s0:s0 + C_LN, ws:ws + C_WIN], -cfull_im[s0:s0 + C_LN, ws:ws + C_WIN]], 0))
    prm['c_bq'] = jnp.stack(bq).astype(BF16)
    prm['c_cq'] = jnp.stack(cq).astype(BF16)
    prm['c_dsk'] = P['c_d'][i][None]
    prm['c_wg'] = P['c_w_glu'][i].astype(BF16)
    prm['c_bg'] = P['c_b_glu'][i][None]
    prm['d_mu'] = P['d_mu'][i][_D_PERM][None]
    for k_, n_ in (('d_w0', 'd_w0'), ('d_a0', 'd_a0'), ('d_kk', 'd_k_k'), ('d_ka', 'd_k_a'),
                   ('d_gnw', 'd_gn_w'), ('d_gnb', 'd_gn_b')):
        prm[k_] = P[n_][i][None]
    prm['d_rk'] = P['d_r_k'][i].reshape(1, BR)
    zl = lambda r: jnp.zeros((r, BR), F32)
    prm['d_gl'] = jnp.concatenate([P['d_g_lora'][i], zl(D_LW + D_LA)], 0).astype(BF16)
    prm['d_wl'] = jnp.concatenate([zl(D_LG), P['d_w_lora'][i], zl(D_LA)], 0).astype(BF16)
    prm['d_al'] = jnp.concatenate([zl(D_LG + D_LW), P['d_a_lora'][i]], 0).astype(BF16)
    prm['d_blk'] = prm['blk64'].astype(BF16)
    return prm


def _layer(x, states, layer, prm, tabs, *, nb, prompt):
    m = x.shape[0]
    l = m // nb
    tm = min(m, 512)
    proj, dt = _norm_mm(x, prm['norm1'], prm['w_in'], prm['w_dt'], tm=tm, tn=1024)
    proj3 = proj.reshape(nb, l, P_COLS)
    dt3 = dt.reshape(nb, l, 128)
    qn, kv = _a_prep(proj, prm['a_qg'], prm['a_kg'], prm['d_blk'], tm=tm)
    if prompt:
        oa = _a_attn_prompt(qn, kv, tabs['a_bias'])
        kv_new = kv[m - min(A_WIN, m):].reshape(nb, min(A_WIN, m), 2, NHEAD, HEAD)
    else:
        oa = _a_attn_sample(qn, kv, states['kv'], layer, tabs['a_sample']).reshape(m, BR)
        kv_new = kv.reshape(nb, l, 2, NHEAD, HEAD)
    ob, conv_new, ssm_new = _mixer_b(proj3, dt3, states['conv'], states['ssm'], layer, prm, q=min(l, 128))
    if prompt:
        oc, hf = _mixer_c_prompt(proj, states['s5'], prm, t=min(l, 256))
        s5_new = jnp.stack([hf[:, :C_LN].reshape(C_NG, C_STATE), hf[:, C_LN:].reshape(C_NG, C_STATE)], -1)[None]
    else:
        u_t = proj3[:, :, P_U:P_U + BR].transpose(1, 0, 2)
        s0 = states['s5'][layer]
        oc_t, hr, hi = _mixer_c_sample(u_t, s0[..., 0].reshape(nb, C_NS), s0[..., 1].reshape(nb, C_NS), prm)
        oc = oc_t.transpose(1, 0, 2).reshape(m, BR)
        s5_new = jnp.stack([hr.reshape(nb, C_NG, C_STATE), hi.reshape(nb, C_NG, C_STATE)], -1)
    od, sh_new, wkv_new = _mixer_d(proj3, states['shift'], states['wkv'], layer, prm, t=min(l, 64))
    shift_new = sh_new.reshape(nb, D_COLS)[:, _D_INV]
    merged = _merge([oa, ob.reshape(m, BR), oc, od.reshape(m, BR)], prm['w_branch'], proj, tm=tm, tn=512)
    x = _mm_res(merged, prm['w_out'], x, tm=tm, tn=1024, tk=D_MODEL)
    hid = _norm_mm(x, prm['norm2'], prm['w_ff1'], tm=tm, tn=1024, relu2=True, out_dtype=BF16)
    x = _mm_res(hid, prm['w_ff2'], x, tm=tm, tn=1024, tk=2048)
    return x, (kv_new, conv_new, ssm_new, s5_new, shift_new, wkv_new)


def kernel(x_prompt, x_sample, cache_kv_a, state_conv, state_ssm, state_s5, state_shift, state_wkv,
           norm1, w_in, a_q_gain, a_k_gain,
           b_conv_w, b_conv_b, b_dt_bias, b_a_log, b_d, b_norm,
           c_a_re, c_a_im, c_log_step, c_b_re, c_b_im, c_c_re, c_c_im, c_d, c_w_glu, c_b_glu,
           d_mu, d_w0, d_w_lora, d_a0, d_a_lora, d_g_lora, d_k_k, d_k_a, d_r_k, d_gn_w, d_gn_b,
           w_branch, w_out, norm2, w_ff1, w_ff2):
    P = dict(norm1=norm1, w_in=w_in, a_q_gain=a_q_gain, a_k_gain=a_k_gain,
             b_conv_w=b_conv_w, b_conv_b=b_conv_b, b_dt_bias=b_dt_bias, b_a_log=b_a_log,
             b_d=b_d, b_norm=b_norm,
             c_a_re=c_a_re, c_a_im=c_a_im, c_log_step=c_log_step, c_b_re=c_b_re, c_b_im=c_b_im,
             c_c_re=c_c_re, c_c_im=c_c_im, c_d=c_d, c_w_glu=c_w_glu, c_b_glu=c_b_glu,
             d_mu=d_mu, d_w0=d_w0, d_w_lora=d_w_lora, d_a0=d_a0, d_a_lora=d_a_lora,
             d_g_lora=d_g_lora, d_k_k=d_k_k, d_k_a=d_k_a, d_r_k=d_r_k, d_gn_w=d_gn_w, d_gn_b=d_gn_b,
             w_branch=w_branch, w_out=w_out, norm2=norm2, w_ff1=w_ff1, w_ff2=w_ff2)
    depth = w_in.shape[0]
    bp, lp, _ = x_prompt.shape
    bs, ls, _ = x_sample.shape
    assert bp == 1 and ls == A_T
    tabs = {'a_bias': _a_band_bias(), 'a_sample': _a_sample_tables(cache_kv_a.shape[2])}
    zero_states = {'conv': jnp.zeros((1, bp, B_CONV - 1, B_CONV_DIM), F32),
                   'ssm': jnp.zeros((1, bp, NHEAD, HEAD, B_STATE), F32),
                   's5': jnp.zeros((C_Q, 2 * C_LN), F32),
                   'shift': jnp.zeros((1, bp, 1, D_COLS), F32),
                   'wkv': jnp.zeros((1, bp, NHEAD, HEAD, HEAD), F32)}
    samp_states = {'kv': jnp.transpose(cache_kv_a, (0, 1, 3, 4, 5, 2)), 'conv': state_conv, 'ssm': state_ssm, 's5': state_s5,
                   'shift': state_shift[:, :, _D_PERM][:, :, None, :], 'wkv': state_wkv}
    yp = x_prompt.reshape(bp * lp, D_MODEL)
    ys = x_sample.reshape(bs * ls, D_MODEL)
    new_p, new_s = [], []
    for i in range(depth):
        prm = _layer_params(i, P)
        yp, st = _layer(yp, zero_states, 0, prm, tabs, nb=bp, prompt=True)
        new_p.append(st)
        ys, st = _layer(ys, samp_states, i, prm, tabs, nb=bs, prompt=False)
        new_s.append(st)
    outs_p = [jnp.stack(z) for z in zip(*new_p)]
    outs_s = [jnp.stack(z) for z in zip(*new_s)]
    res = [yp.reshape(bp, lp, D_MODEL), ys.reshape(bs, ls, D_MODEL)]
    for a, b in zip(outs_p, outs_s):
        res += [a, b]
    return tuple(res)
```

```python
import functools

import jax
import jax.numpy as jnp
import numpy as np
from jax import lax
from jax.experimental import pallas as pl
from jax.experimental.pallas import tpu as pltpu

F32 = jnp.float32
BF16 = jnp.bfloat16
HI = lax.Precision.HIGHEST

D_MODEL = 2048
BR = 768
N_BRANCH = 4
HEAD = 64
NHEAD = BR // HEAD
A_WIN = 2048
A_BLK = 128
A_NKB = A_WIN // A_BLK + 1
A_PATTERNS = ((128, 1), (512, 4), (2048, 16))
B_GROUPS = 4
B_STATE = 128
B_CONV = 4
B_BC = B_GROUPS * B_STATE
B_CONV_DIM = BR + 2 * B_BC
C_GROUP = 16
C_NG = BR // C_GROUP
C_STATE = 64
C_NS = C_NG * C_STATE
C_Q = 8
C_LN = C_NS // C_Q
C_WIN = 256
D_LW, D_LA, D_LG = 64, 64, 128
D_LORA = D_LW + D_LA + D_LG
D_COLS = 3 * BR + D_LORA
D_FF = 4 * D_MODEL
NORM_EPS = 1e-6
RWKV_GN_EPS = 64e-5
NEG = -1e30

P_Q, P_K, P_V, P_Z, P_U, P_X, P_R, P_DK, P_DV = (BR * n for n in range(9))
P_DL = 9 * BR
P_BM = P_DL + D_LORA
P_CM = P_BM + B_BC
P_GATE = P_CM + B_BC
P_COLS = P_GATE + N_BRANCH * D_MODEL
VMEM_LIMIT = 56 * 1024 * 1024
D_CHUNK = 64
D_PASSES = (1, 1, 3)


def _cparams(*sem):
    return pltpu.CompilerParams(dimension_semantics=sem, vmem_limit_bytes=VMEM_LIMIT)


def _mm(a, b, prec=None):
    return lax.dot_general(a, b, (((1,), (0,)), ((), ())), precision=prec, preferred_element_type=F32)


def _mm_nt(a, b, prec=None):
    return lax.dot_general(a, b, (((1,), (1,)), ((), ())), precision=prec, preferred_element_type=F32)


def _mm_tn(a, b, prec=None):
    return lax.dot_general(a, b, (((0,), (0,)), ((), ())), precision=prec, preferred_element_type=F32)


_DIMS = {'nn': (((1,), (0,)), ((), ())), 'nt': (((1,), (1,)), ((), ())), 'tn': (((0,), (0,)), ((), ()))}


def _split(x):
    hi = x.astype(BF16)
    return hi, (x - hi.astype(F32)).astype(BF16)


def _pdot(a, b, kind, passes):
    def dg(x, y, prec=None):
        return lax.dot_general(x, y, _DIMS[kind], precision=prec, preferred_element_type=F32)

    if passes == 6:
        return dg(a, b, HI)
    if passes == 1:
        return dg(a.astype(BF16), b.astype(BF16))
    ah, al = _split(a)
    bh, bl = _split(b)
    return dg(ah, bh) + (dg(ah, bl) + dg(al, bh))


def _sel_mm(x, sel_bf16, terms):
    acc = None
    for _ in range(terms):
        hi = x.astype(BF16)
        t = _mm(hi, sel_bf16)
        acc = t if acc is None else acc + t
        x = x - hi.astype(F32)
    return acc


def _bmm(a, b):
    return _mm(a.astype(BF16), b.astype(BF16))


def _bmm_nt(a, b):
    return _mm_nt(a.astype(BF16), b.astype(BF16))


def _bmm_tn(a, b):
    return _mm_tn(a.astype(BF16), b.astype(BF16))


def _iota(shape, dim):
    return lax.broadcasted_iota(jnp.int32, shape, dim)


def _sigmoid(x):
    return 1.0 / (1.0 + jnp.exp(-x))


def _softplus(x):
    return jnp.maximum(x, 0.0) + jnp.log(1.0 + jnp.exp(-jnp.abs(x)))


def _silu(x):
    return x * _sigmoid(x)


def _const_spec(a, ngrid):
    nd = a.ndim
    return pl.BlockSpec(a.shape, lambda *_: (0,) * nd)


def _norm_mm_kernel(x_ref, g_ref, w_ref, *rest, relu2, with_dt):
    if with_dt:
        wdt_ref, o_ref, odt_ref, h_scr = rest
    else:
        o_ref, h_scr = rest
    mm = _mm_nt if with_dt else _mm

    @pl.when(pl.program_id(1) == 0)
    def _():
        x = x_ref[...]
        ms = jnp.mean(x * x, axis=-1, keepdims=True)
        h = (x * lax.rsqrt(ms + NORM_EPS) * g_ref[...]).astype(BF16)
        h_scr[...] = h
        if with_dt:
            odt_ref[...] = mm(h, wdt_ref[...])

    y = mm(h_scr[...], w_ref[...])
    if relu2:
        y = jnp.square(jnp.maximum(y, 0.0))
    o_ref[...] = y.astype(o_ref.dtype)


def _norm_mm(x, g, w, wdt=None, *, tm, tn, relu2=False, out_dtype=F32):
    m, k = x.shape
    with_dt = wdt is not None
    n = w.shape[0] if with_dt else w.shape[1]
    in_specs = [pl.BlockSpec((tm, k), lambda i, j: (i, 0)),
                pl.BlockSpec((1, k), lambda i, j: (0, 0)),
                pl.BlockSpec((tn, k), lambda i, j: (j, 0)) if with_dt else pl.BlockSpec((k, tn), lambda i, j: (0, j))]
    out_specs = pl.BlockSpec((tm, tn), lambda i, j: (i, j))
    out_shape = jax.ShapeDtypeStruct((m, n), out_dtype)
    args = [x, g, w]
    if with_dt:
        in_specs.append(pl.BlockSpec((128, k), lambda i, j: (0, 0)))
        out_specs = [out_specs, pl.BlockSpec((tm, 128), lambda i, j: (i, 0))]
        out_shape = [out_shape, jax.ShapeDtypeStruct((m, 128), F32)]
        args.append(wdt)
    return pl.pallas_call(
        functools.partial(_norm_mm_kernel, relu2=relu2, with_dt=with_dt),
        grid=(m // tm, n // tn), in_specs=in_specs, out_specs=out_specs, out_shape=out_shape,
        scratch_shapes=[pltpu.VMEM((tm, k), BF16)],
        compiler_params=_cparams("parallel", "arbitrary"),
    )(*args)


def _mm_res_kernel(a_ref, w_ref, r_ref, o_ref):
    @pl.when(pl.program_id(2) == 0)
    def _():
        o_ref[...] = r_ref[...]

    o_ref[...] += _mm(a_ref[...], w_ref[...])


def _mm_res(a, w, res, *, tm, tn, tk):
    m, k = a.shape
    n = w.shape[1]
    return pl.pallas_call(
        _mm_res_kernel, grid=(m // tm, n // tn, k // tk),
        in_specs=[pl.BlockSpec((tm, tk), lambda i, j, kk: (i, kk)),
                  pl.BlockSpec((tk, tn), lambda i, j, kk: (kk, j)),
                  pl.BlockSpec((tm, tn), lambda i, j, kk: (i, j))],
        out_specs=pl.BlockSpec((tm, tn), lambda i, j, kk: (i, j)),
        out_shape=jax.ShapeDtypeStruct((m, n), F32),
        compiler_params=_cparams("parallel", "parallel", "arbitrary"),
    )(a, w, res)


def _merge_kernel(oa, ob, oc, od, wb, g0, g1, g2, g3, o_ref):
    acc = None
    for n, (o, g) in enumerate(((oa, g0), (ob, g1), (oc, g2), (od, g3))):
        t = _sigmoid(g[...]) * _mm(o[...], wb[n])
        acc = t if acc is None else acc + t
    o_ref[...] = acc.astype(o_ref.dtype)


def _merge(outs, wb, proj, *, tm, tn):
    m = proj.shape[0]
    o_spec = pl.BlockSpec((tm, BR), lambda i, j: (i, 0))

    def g_spec(n):
        return pl.BlockSpec((tm, tn), lambda i, j: (i, (P_GATE + n * D_MODEL) // tn + j))

    return pl.pallas_call(
        _merge_kernel, grid=(m // tm, D_MODEL // tn),
        in_specs=[o_spec] * 4 + [pl.BlockSpec((N_BRANCH, BR, tn), lambda i, j: (0, 0, j))]
        + [g_spec(n) for n in range(N_BRANCH)],
        out_specs=pl.BlockSpec((tm, tn), lambda i, j: (i, j)),
        out_shape=jax.ShapeDtypeStruct((m, D_MODEL), BF16),
        compiler_params=_cparams("parallel", "parallel"),
    )(*outs, wb, proj, proj, proj, proj)


def _a_prep_kernel(q_ref, k_ref, v_ref, qg_ref, kg_ref, blk_ref, qn_ref, kv_ref):
    blk = blk_ref[...]
    q = q_ref[...]
    k = k_ref[...]
    qms = _sel_mm(q * q, blk, 3) * (1.0 / HEAD)
    kms = _sel_mm(k * k, blk, 3) * (1.0 / HEAD)
    qn_ref[...] = q * lax.rsqrt(qms + NORM_EPS) * qg_ref[...] * (HEAD ** -0.5)
    kv_ref[:, :BR] = k * lax.rsqrt(kms + NORM_EPS) * kg_ref[...]
    kv_ref[:, BR:] = v_ref[...]


def _a_prep(proj, qg, kg, blk, *, tm):
    m = proj.shape[0]

    def col(c):
        return pl.BlockSpec((tm, BR), lambda i: (i, c))

    vec = pl.BlockSpec((1, BR), lambda i: (0, 0))
    return pl.pallas_call(
        _a_prep_kernel, grid=(m // tm,),
        in_specs=[col(P_Q // BR), col(P_K // BR), col(P_V // BR), vec, vec, pl.BlockSpec((BR, BR), lambda i: (0, 0))],
        out_specs=[pl.BlockSpec((tm, BR), lambda i: (i, 0)), pl.BlockSpec((tm, 2 * BR), lambda i: (i, 0))],
        out_shape=[jax.ShapeDtypeStruct((m, BR), F32), jax.ShapeDtypeStruct((m, 2 * BR), F32)],
        compiler_params=_cparams("parallel"),
    )(proj, proj, proj, qg, kg, blk)


def _a_band_bias():
    a = np.arange(A_BLK)[:, None]
    c = np.arange(2 * A_BLK)[None, :]
    band = (c >= a) & (c <= a + A_BLK)
    return jnp.asarray(np.stack([np.where(band, 0.0, NEG), np.where(band & (c >= A_BLK), 0.0, NEG)]).astype(np.float32))


def _a_attn_kernel(q_ref, kp_ref, kc_ref, vp_ref, vc_ref, bias_ref, o_ref, k_scr, v_scr, m_scr, l_scr, acc_scr,
                   *, sb_len):
    sb = pl.program_id(1)
    k_scr[0:sb_len, :] = kp_ref[...]
    k_scr[sb_len:, :] = kc_ref[...]
    v_scr[0:sb_len, :] = vp_ref[...]
    v_scr[sb_len:, :] = vc_ref[...]
    lane = _iota((A_BLK, 2 * HEAD), 1)
    left = lane < HEAD
    first_sb = sb == 0
    for pi_, (w, d) in enumerate(A_PATTERNS):
        assert w // d == A_BLK and sb_len % (A_BLK * d) == 0
        for j in range(sb_len // (A_BLK * d)):
            for r in range(d):
                q0 = r + d * A_BLK * j
                k0 = sb_len + q0 - d * A_BLK
                rows_q = pl.ds(q0, A_BLK, stride=d) if d > 1 else pl.ds(q0, A_BLK)
                rows_k = pl.ds(k0, 2 * A_BLK, stride=d) if d > 1 else pl.ds(k0, 2 * A_BLK)
                q = q_ref[rows_q, :]
                kt = k_scr[rows_k, :].astype(BF16)
                vt = v_scr[rows_k, :].astype(BF16)
                if j == 0:
                    bias = jnp.where(first_sb, bias_ref[1], bias_ref[0])
                else:
                    bias = bias_ref[0]
                ms, ls, accs = [], [], []
                for h in range(2):
                    qh = jnp.where(left if h == 0 else ~left, q, 0.0).astype(BF16)
                    s = _mm_nt(qh, kt) + bias
                    mu = jnp.max(s, axis=-1, keepdims=True)
                    p = jnp.exp(s - mu)
                    ms.append(mu)
                    ls.append(jnp.sum(p, axis=-1, keepdims=True))
                    accs.append(_mm(p.astype(BF16), vt))
                mu = jnp.where(left, ms[0], ms[1])
                lu = jnp.where(left, ls[0], ls[1])
                au = jnp.where(left, accs[0], accs[1])
                if pi_ == 0:
                    m_scr[rows_q, :] = mu
                    l_scr[rows_q, :] = lu
                    acc_scr[rows_q, :] = au
                else:
                    mo = m_scr[rows_q, :]
                    mn = jnp.maximum(mo, mu)
                    eo = jnp.exp(mo - mn)
                    eu = jnp.exp(mu - mn)
                    m_scr[rows_q, :] = mn
                    l_scr[rows_q, :] = eo * l_scr[rows_q, :] + eu * lu
                    acc_scr[rows_q, :] = eo * acc_scr[rows_q, :] + eu * au
    o_ref[...] = (acc_scr[...] / l_scr[...]).astype(o_ref.dtype)


def _a_attn_prompt(qn, kv, bias):
    l = qn.shape[0]
    npair = NHEAD // 2
    sb_len = min(l, A_WIN)
    blk = (sb_len, 2 * HEAD)
    prev = lambda hp, s: (jnp.maximum(s - 1, 0), hp)
    prev_v = lambda hp, s: (jnp.maximum(s - 1, 0), npair + hp)
    return pl.pallas_call(
        functools.partial(_a_attn_kernel, sb_len=sb_len), grid=(npair, l // sb_len),
        in_specs=[pl.BlockSpec(blk, lambda hp, s: (s, hp)),
                  pl.BlockSpec(blk, prev), pl.BlockSpec(blk, lambda hp, s: (s, hp)),
                  pl.BlockSpec(blk, prev_v), pl.BlockSpec(blk, lambda hp, s: (s, npair + hp)),
                  _const_spec(bias, 2)],
        out_specs=pl.BlockSpec(blk, lambda hp, s: (s, hp)),
        out_shape=jax.ShapeDtypeStruct((l, BR), BF16),
        scratch_shapes=[pltpu.VMEM((2 * sb_len, 2 * HEAD), F32), pltpu.VMEM((2 * sb_len, 2 * HEAD), F32)]
        + [pltpu.VMEM(blk, F32)] * 3,
        compiler_params=_cparams("parallel", "arbitrary"),
    )(qn, kv, kv, kv, kv, bias)


A_T = 4


def _a_sample_tables(past):
    row_t = np.arange(64)[:, None] // 16
    d = past + row_t - np.arange(past)[None, :]
    mc = np.zeros(d.shape, np.float32)
    for w, dil in A_PATTERNS:
        mc += ((d >= 0) & (d <= w) & (d % dil == 0)).astype(np.float32)
    tk = np.arange(8)[None, :]
    mn = np.where((tk <= row_t) & (tk < A_T), 1.0 + (len(A_PATTERNS) - 1.0) * (tk == row_t), 0.0).astype(np.float32)
    hm = np.zeros((16, BR), np.float32)
    for h in range(NHEAD):
        hm[h, h * HEAD:(h + 1) * HEAD] = 1.0
    return jnp.asarray(mc), jnp.asarray(mn), jnp.asarray(hm)


def _a_sample_kernel(q_ref, kvn_ref, c_ref, hm_ref, mc_ref, mn_ref, o_ref, new_scr, *, past):
    hm = hm_ref[...]
    q = q_ref[0]
    qexp = jnp.concatenate([jnp.broadcast_to(q[t:t + 1, :], (16, BR)) * hm for t in range(A_T)], axis=0)
    qexp = qexp.astype(BF16)
    new_scr[...] = jnp.zeros_like(new_scr)
    new_scr[0:A_T, :] = kvn_ref[0]
    kt = c_ref[0, 0, 0].reshape(BR, past).astype(BF16)
    vt = c_ref[0, 0, 1].reshape(BR, past).astype(BF16)
    mc = mc_ref[...]
    mn = mn_ref[...]
    s = jnp.where(mc > 0.0, _mm(qexp, kt), NEG)
    sn = jnp.where(mn > 0.0, _mm_nt(qexp, new_scr[:, :BR].astype(BF16)), NEG)
    m = jnp.maximum(jnp.max(s, axis=-1, keepdims=True), jnp.max(sn, axis=-1, keepdims=True))
    p = mc * jnp.exp(s - m)
    pn = mn * jnp.exp(sn - m)
    den = jnp.sum(p, axis=-1, keepdims=True) + jnp.sum(pn, axis=-1, keepdims=True)
    r = (_mm_nt(p.astype(BF16), vt) + _mm(pn.astype(BF16), new_scr[:, BR:].astype(BF16))) / den
    for t in range(A_T):
        o_ref[0, t:t + 1, :] = jnp.sum(r[16 * t:16 * t + 16, :] * hm, axis=0, keepdims=True).astype(o_ref.dtype)


def _a_attn_sample(qn, kv_new, cache_t, layer, tables):
    nb, past = cache_t.shape[1], cache_t.shape[-1]
    mc, mn, hm = tables
    return pl.pallas_call(
        functools.partial(_a_sample_kernel, past=past), grid=(nb,),
        in_specs=[pl.BlockSpec((1, A_T, BR), lambda b: (b, 0, 0)),
                  pl.BlockSpec((1, A_T, 2 * BR), lambda b: (b, 0, 0)),
                  pl.BlockSpec((1, 1, 2, NHEAD, HEAD, past), lambda b: (layer, b, 0, 0, 0, 0)),
                  _const_spec(hm, 1), _const_spec(mc, 1), _const_spec(mn, 1)],
        out_specs=pl.BlockSpec((1, A_T, BR), lambda b: (b, 0, 0)),
        out_shape=jax.ShapeDtypeStruct((nb, A_T, BR), BF16),
        scratch_shapes=[pltpu.VMEM((8, 2 * BR), F32)],
        compiler_params=_cparams("parallel"),
    )(qn.reshape(nb, A_T, BR), kv_new.reshape(nb, A_T, 2 * BR), cache_t, hm, mc, mn)


def _b_kernel(z_ref, x_ref, bm_ref, cm_ref, dt_ref, conv0_ref, ssm0_ref, cw_ref, cb_ref, dtb_ref, a_ref, dsk_ref,
              nw_ref, sel_ref, gblk_ref, y_ref, conv_ref, ssm_ref, xp_scr, s_scr, pad_scr, *, q, qp):
    c = pl.program_id(1)

    @pl.when(c == 0)
    def _():
        xp_scr[...] = jnp.zeros_like(xp_scr)
        xp_scr[5:8, :] = conv0_ref[0, 0]
        s_scr[...] = ssm0_ref[0, 0]

    xp_scr[8:8 + q, 0:BR] = x_ref[0]
    xp_scr[8:8 + q, BR:BR + B_BC] = bm_ref[0]
    xp_scr[8:8 + q, BR + B_BC:] = cm_ref[0]
    cw = cw_ref[...]
    conv = (cb_ref[...] + cw[3:4] * xp_scr[8:8 + qp, :] + cw[2:3] * xp_scr[7:7 + qp, :]
            + cw[1:2] * xp_scr[6:6 + qp, :] + cw[0:1] * xp_scr[5:5 + qp, :])
    tail = xp_scr[5 + q:8 + q, :]
    xp_scr[5:8, :] = tail
    conv_ref[0] = tail
    u = _silu(conv)
    xs = u[:, :BR]
    dt = _softplus(dt_ref[0] + dtb_ref[...])
    z = z_ref[0]
    if q != qp:
        pad_scr[...] = jnp.zeros_like(pad_scr)
        pad_scr[0:q, :BR] = z
        pad_scr[0:q, BR:] = dt
        z = pad_scr[:, :BR]
        dt = pad_scr[:, BR:]
    adt = dt * a_ref[...]
    trib = _iota((qp, qp), 0) >= _iota((qp, qp), 1)
    acum = _mm(trib.astype(F32), adt, HI)
    acum_t = _mm_nt(sel_ref[...], acum, HI)
    alast = acum[qp - 1:qp, :]
    dend = jnp.exp(alast - acum)
    eacum = jnp.exp(acum)
    elast = jnp.exp(alast)
    ys = []
    hpg = NHEAD // B_GROUPS
    for g in range(B_GROUPS):
        bm = u[:, BR + g * B_STATE:BR + (g + 1) * B_STATE]
        cm = u[:, BR + B_BC + g * B_STATE:BR + B_BC + (g + 1) * B_STATE]
        gmat = _bmm_nt(cm, bm)
        for h in range(g * hpg, (g + 1) * hpg):
            xh = xs[:, h * HEAD:(h + 1) * HEAD]
            xdt = xh * dt[:, h:h + 1]
            lmat = jnp.where(trib, jnp.exp(acum[:, h:h + 1] - acum_t[h:h + 1, :]), 0.0)
            s_h = s_scr[h]
            ys.append(_bmm(gmat * lmat, xdt) + eacum[:, h:h + 1] * _bmm_nt(cm, s_h) + xh * dsk_ref[:, h:h + 1])
            s_scr[h] = elast[:, h:h + 1] * s_h + _bmm_tn(xdt * dend[:, h:h + 1], bm)
    y = jnp.concatenate(ys, axis=-1) * _silu(z)
    ms = _mm(y * y, gblk_ref[...], HI)
    y = y * lax.rsqrt(ms + NORM_EPS) * nw_ref[...]
    y_ref[0] = y[:q].astype(y_ref.dtype)
    ssm_ref[0] = s_scr[...]


def _mixer_b(proj3, dt3, conv0, ssm0, layer, prm, *, q):
    nb, l, _ = proj3.shape
    qp = max(q, 8)
    consts = [prm[k] for k in ('b_cw', 'b_cb', 'b_dtb', 'b_a', 'b_dsk', 'b_nw', 'b_sel', 'b_gblk')]
    return pl.pallas_call(
        functools.partial(_b_kernel, q=q, qp=qp), grid=(nb, l // q),
        in_specs=[pl.BlockSpec((1, q, BR), lambda b, c: (b, c, P_Z // BR)),
                  pl.BlockSpec((1, q, BR), lambda b, c: (b, c, P_X // BR)),
                  pl.BlockSpec((1, q, B_BC), lambda b, c: (b, c, P_BM // B_BC)),
                  pl.BlockSpec((1, q, B_BC), lambda b, c: (b, c, P_CM // B_BC)),
                  pl.BlockSpec((1, q, 128), lambda b, c: (b, c, 0)),
                  pl.BlockSpec((1, 1, B_CONV - 1, B_CONV_DIM), lambda b, c: (layer, b, 0, 0)),
                  pl.BlockSpec((1, 1, NHEAD, HEAD, B_STATE), lambda b, c: (layer, b, 0, 0, 0))]
        + [_const_spec(a, 2) for a in consts],
        out_specs=[pl.BlockSpec((1, q, BR), lambda b, c: (b, c, 0)),
                   pl.BlockSpec((1, B_CONV - 1, B_CONV_DIM), lambda b, c: (b, 0, 0)),
                   pl.BlockSpec((1, NHEAD, HEAD, B_STATE), lambda b, c: (b, 0, 0, 0))],
        out_shape=[jax.ShapeDtypeStruct((nb, l, BR), BF16),
                   jax.ShapeDtypeStruct((nb, B_CONV - 1, B_CONV_DIM), F32),
                   jax.ShapeDtypeStruct((nb, NHEAD, HEAD, B_STATE), F32)],
        scratch_shapes=[pltpu.VMEM((qp + 8, B_CONV_DIM), F32), pltpu.VMEM((NHEAD, HEAD, B_STATE), F32),
                        pltpu.VMEM((qp, BR + 128), F32)],
        compiler_params=_cparams("parallel", "arbitrary"),
    )(proj3, proj3, proj3, proj3, dt3, conv0, ssm0, *consts)


def _gelu_tanh(y):
    return 0.5 * y * (1.0 + jnp.tanh(0.7978845608028654 * (y + 0.044715 * (y * y * y))))


def _c_window(qq):
    first = qq * C_LN // C_STATE * C_GROUP
    return min(first // 128 * 128, BR - C_WIN)


def _c_prompt_kernel(u_ref, h0_ref, lb_ref, bq_ref, cq_ref, dsk_ref, wg_ref, bg_ref, o_ref, hf_ref,
                     bu_scr, hs_scr, h_scr, *, t):
    @pl.when(pl.program_id(0) == 0)
    def _():
        h_scr[...] = h0_ref[...]

    u = u_ref[...]
    ub = u.astype(BF16)
    ntile = 2 * C_LN // 128
    for qq in range(C_Q):
        ws = _c_window(qq)
        bu = _mm(ub[:, ws:ws + C_WIN], bq_ref[qq])
        for j in range(ntile):
            bu_scr[j, pl.ds(qq, t, stride=C_Q), :] = bu[:, j * 128:(j + 1) * 128]
    lb = [lb_ref[:, j * 128:(j + 1) * 128] for j in range(ntile)]
    half = ntile // 2

    def step(i, h):
        r0 = pl.multiple_of(i * C_Q, C_Q)
        new = []
        for j in range(half):
            new.append(lb[j] * h[j] - lb[half + j] * h[half + j] + bu_scr[j, pl.ds(r0, C_Q), :])
        for j in range(half):
            new.append(lb[j] * h[half + j] + lb[half + j] * h[j] + bu_scr[half + j, pl.ds(r0, C_Q), :])
        for j in range(ntile):
            hs_scr[j, pl.ds(r0, C_Q), :] = new[j]
        return tuple(new)

    h = lax.fori_loop(0, t, step, tuple(h_scr[:, j * 128:(j + 1) * 128] for j in range(ntile)), unroll=8)
    for j in range(ntile):
        h_scr[:, j * 128:(j + 1) * 128] = h[j]
    hf_ref[...] = h_scr[...]
    ytile = [None] * (BR // 128)
    for qq in range(C_Q):
        ws = _c_window(qq) // 128
        hq = jnp.concatenate([hs_scr[j, pl.ds(qq, t, stride=C_Q), :] for j in range(ntile)], axis=-1)
        yq = _mm(hq.astype(BF16), cq_ref[qq])
        for j in range(C_WIN // 128):
            part = yq[:, j * 128:(j + 1) * 128]
            ytile[ws + j] = part if ytile[ws + j] is None else ytile[ws + j] + part
    y = _gelu_tanh(jnp.concatenate(ytile, axis=-1) + dsk_ref[...] * u)
    o_ref[...] = (y * _sigmoid(_bmm(y, wg_ref[...]) + bg_ref[...])).astype(o_ref.dtype)


def _mixer_c_prompt(proj, h0, prm, *, t):
    l = proj.shape[0]
    consts = [prm[k] for k in ('c_lb8', 'c_bq', 'c_cq', 'c_dsk', 'c_wg', 'c_bg')]
    return pl.pallas_call(
        functools.partial(_c_prompt_kernel, t=t), grid=(l // t,),
        in_specs=[pl.BlockSpec((t, BR), lambda c: (c, P_U // BR)), _const_spec(h0, 1)]
        + [_const_spec(a, 1) for a in consts],
        out_specs=[pl.BlockSpec((t, BR), lambda c: (c, 0)), pl.BlockSpec((C_Q, 2 * C_LN), lambda c: (0, 0))],
        out_shape=[jax.ShapeDtypeStruct((l, BR), BF16), jax.ShapeDtypeStruct((C_Q, 2 * C_LN), F32)],
        scratch_shapes=[pltpu.VMEM((2 * C_LN // 128, C_Q * t, 128), F32),
                        pltpu.VMEM((2 * C_LN // 128, C_Q * t, 128), F32),
                        pltpu.VMEM((C_Q, 2 * C_LN), F32)],
        compiler_params=_cparams("arbitrary"),
    )(proj, h0, *consts)


def _c_sample_kernel(u_ref, hr0_ref, hi0_ref, lbr_ref, lbi_ref, bf_ref, cf_ref, dsk_ref, wg_ref, bg_ref,
                     o_ref, hr_ref, hi_ref, *, nt):
    hr = hr0_ref[...]
    hi = hi0_ref[...]
    lbr = lbr_ref[...]
    lbi = lbi_ref[...]
    for t in range(nt):
        u = u_ref[t]
        bu = _bmm(u, bf_ref[...])
        hr, hi = lbr * hr - lbi * hi + bu[:, :C_NS], lbr * hi + lbi * hr + bu[:, C_NS:]
        y = _bmm(hr, cf_ref[:C_NS, :]) + _bmm(hi, cf_ref[C_NS:, :]) + dsk_ref[...] * u
        y = _gelu_tanh(y)
        o_ref[t] = (y * _sigmoid(_bmm(y, wg_ref[...]) + bg_ref[...])).astype(o_ref.dtype)
    hr_ref[...] = hr
    hi_ref[...] = hi


def _mixer_c_sample(u_t, hr0, hi0, prm):
    nt, nb, _ = u_t.shape
    args = [u_t, hr0, hi0] + [prm[k] for k in ('c_lbr', 'c_lbi', 'c_bf', 'c_cf', 'c_dsk', 'c_wg', 'c_bg')]
    return pl.pallas_call(
        functools.partial(_c_sample_kernel, nt=nt), grid=(1,),
        in_specs=[_const_spec(a, 1) for a in args],
        out_specs=[pl.BlockSpec((nt, nb, BR), lambda c: (0, 0, 0)), pl.BlockSpec((nb, C_NS), lambda c: (0, 0)),
                   pl.BlockSpec((nb, C_NS), lambda c: (0, 0))],
        out_shape=[jax.ShapeDtypeStruct((nt, nb, BR), BF16), jax.ShapeDtypeStruct((nb, C_NS), F32),
                   jax.ShapeDtypeStruct((nb, C_NS), F32)],
        compiler_params=_cparams("arbitrary"),
    )(*args)


def _d_kernel(r_ref, k_ref, v_ref, l_ref, sh0_ref, wkv0_ref, mu_ref, w0_ref, a0_ref, kk_ref, ka_ref, rk_ref,
              gnw_ref, gnb_ref, wl_ref, al_ref, gl_ref, blk_ref, y_ref, sh_ref, wkv_ref, x_scr, ht_scr, *, t, tp,
              pa, pi, ps):
    c = pl.program_id(1)

    @pl.when(c == 0)
    def _():
        x_scr[...] = jnp.zeros_like(x_scr)
        x_scr[7:8, :] = sh0_ref[0, 0]
        zero = jnp.zeros((HEAD, HEAD), F32)
        for p in range(NHEAD // 2):
            ht_scr[p, 0:HEAD, :] = jnp.concatenate([wkv0_ref[0, 0, 2 * p], zero], axis=-1)
            ht_scr[p, HEAD:, :] = jnp.concatenate([zero, wkv0_ref[0, 0, 2 * p + 1]], axis=-1)

    x_scr[8:8 + t, 0:BR] = r_ref[0]
    x_scr[8:8 + t, BR:2 * BR] = k_ref[0]
    x_scr[8:8 + t, 2 * BR:3 * BR] = v_ref[0]
    x_scr[8:8 + t, 3 * BR:] = l_ref[0]
    cur = x_scr[8:8 + tp, :]
    xm = cur + (x_scr[7:7 + tp, :] - cur) * mu_ref[...]
    last = x_scr[7 + t:8 + t, :]
    x_scr[7:8, :] = last
    sh_ref[0] = last
    r = xm[:, :BR]
    k = xm[:, BR:2 * BR]
    v = xm[:, 2 * BR:3 * BR]
    xl = xm[:, 3 * BR:]
    lw = -jnp.exp(-_softplus(-(w0_ref[...] + _bmm(jnp.tanh(xl), wl_ref[...]))) - 0.5)
    a = _sigmoid(a0_ref[...] + _bmm(xl, al_ref[...]))
    g = _bmm(_sigmoid(xl), gl_ref[...])
    blk = blk_ref[...]
    kkr = k * kk_ref[...]
    kk = kkr / jnp.maximum(jnp.sqrt(_sel_mm(kkr * kkr, blk, 3)), 1e-12)
    k2 = k * (1.0 + (a - 1.0) * ka_ref[...])
    if t != tp:
        live = (_iota((tp, 1), 0) < t).astype(F32)
        r, k2, v, kk, lw = r * live, k2 * live, v * live, kk * live, lw * live
    bv = kk * a
    row = _iota((tp, tp), 0)
    col = _iota((tp, tp), 1)
    cum = _pdot((row >= col).astype(F32), lw, 'nn', 6)
    cend = cum[tp - 1:tp, :]
    cmid = cum[tp // 2 - 1:tp // 2, :]
    e_in = jnp.exp(cum - cmid)
    e_neg = jnp.exp(cmid - cum)
    e_end = jnp.exp(cend - cum)
    rd = r * jnp.exp(cum)
    kkd = kk * jnp.exp(cum - lw)
    rd_c = r * e_in
    kkd_c = kk * jnp.exp(cum - lw - cmid)
    ks = k2 * e_neg
    bs = bv * e_neg
    kse = k2 * e_end
    bse = bv * e_end
    dend = jnp.exp(cend)
    tp2 = 2 * tp
    row2 = _iota((tp2, tp2), 0) % tp
    col2 = _iota((tp2, tp2), 1) % tp
    incl = row2 >= col2
    strict = row2 > col2
    eye = (_iota((tp2, tp2), 0) == _iota((tp2, tp2), 1)).astype(F32)
    left = _iota((tp, 2 * HEAD), 1) < HEAD

    def stack(x, p):
        xp = x[:, 2 * HEAD * p:2 * HEAD * (p + 1)]
        return jnp.concatenate([jnp.where(left, xp, 0.0), jnp.where(left, 0.0, xp)], axis=0)

    ys = []
    for p in range(NHEAD // 2):
        kkd_s, bs_s, ks_s, rd_s, v_s = stack(kkd, p), stack(bs, p), stack(ks, p), stack(rd, p), stack(v, p)
        kkd_cs, rd_cs = stack(kkd_c, p), stack(rd_c, p)
        akb = jnp.where(strict, _pdot(kkd_cs, bs_s, 'nt', pa), 0.0)
        akk = jnp.where(strict, _pdot(kkd_cs, ks_s, 'nt', pa), 0.0)
        arb = jnp.where(incl, _pdot(rd_cs, bs_s, 'nt', pa), 0.0)
        ark = jnp.where(incl, _pdot(rd_cs, ks_s, 'nt', pa), 0.0)
        inv = eye - akb
        pw = _pdot(akb, akb, 'nn', pi)
        n = 2
        while n < tp:
            inv = inv + _pdot(inv, pw, 'nn', pi)
            n *= 2
            if n < tp:
                pw = _pdot(pw, pw, 'nn', pi)
        w1 = _pdot(inv, kkd_s, 'nn', pi)
        w2 = _pdot(inv, _pdot(akk, v_s, 'nn', pa), 'nn', pi)
        ht = ht_scr[p]
        u = _pdot(w1, ht, 'nt', ps) + w2
        y_s = _pdot(rd_s, ht, 'nt', ps) + _pdot(ark, v_s, 'nn', pa) - _pdot(arb, u, 'nn', pa)
        ys.append(y_s[:tp] + y_s[tp:])
        ht_scr[p] = (ht * dend[:, 2 * HEAD * p:2 * HEAD * (p + 1)] + _pdot(v_s, stack(kse, p), 'tn', ps)
                     - _pdot(u, stack(bse, p), 'tn', ps))
    y = jnp.concatenate(ys, axis=-1)
    mean = _sel_mm(y, blk, 3) * (1.0 / HEAD)
    d = y - mean
    var = _sel_mm(d * d, blk, 3) * (1.0 / HEAD)
    yn = d * lax.rsqrt(var + RWKV_GN_EPS) * gnw_ref[...] + gnb_ref[...]
    bonus = _sel_mm(r * k2 * rk_ref[...], blk, 3)
    y_ref[0] = (((yn + bonus * v) * g)[:t]).astype(y_ref.dtype)

    @pl.when(c == pl.num_programs(1) - 1)
    def _():
        for p in range(NHEAD // 2):
            wkv_ref[0, 2 * p] = ht_scr[p, 0:HEAD, 0:HEAD]
            wkv_ref[0, 2 * p + 1] = ht_scr[p, HEAD:, HEAD:]


def _mixer_d(proj3, shift0, wkv0, layer, prm, *, t):
    nb, l, _ = proj3.shape
    tp = max(t, 8)
    consts = [prm[k] for k in ('d_mu', 'd_w0', 'd_a0', 'd_kk', 'd_ka', 'd_rk', 'd_gnw', 'd_gnb',
                               'd_wl', 'd_al', 'd_gl', 'd_blk')]
    return pl.pallas_call(
        functools.partial(_d_kernel, t=t, tp=tp, pa=D_PASSES[0], pi=D_PASSES[1], ps=D_PASSES[2]),
        grid=(nb, l // t),
        in_specs=[pl.BlockSpec((1, t, BR), lambda b, c: (b, c, P_R // BR)),
                  pl.BlockSpec((1, t, BR), lambda b, c: (b, c, P_DK // BR)),
                  pl.BlockSpec((1, t, BR), lambda b, c: (b, c, P_DV // BR)),
                  pl.BlockSpec((1, t, D_LORA), lambda b, c: (b, c, P_DL // D_LORA)),
                  pl.BlockSpec((1, 1, 1, D_COLS), lambda b, c: (layer, b, 0, 0)),
                  pl.BlockSpec((1, 1, NHEAD, HEAD, HEAD), lambda b, c: (layer, b, 0, 0, 0))]
        + [_const_spec(a, 2) for a in consts],
        out_specs=[pl.BlockSpec((1, t, BR), lambda b, c: (b, c, 0)),
                   pl.BlockSpec((1, 1, D_COLS), lambda b, c: (b, 0, 0)),
                   pl.BlockSpec((1, NHEAD, HEAD, HEAD), lambda b, c: (b, 0, 0, 0))],
        out_shape=[jax.ShapeDtypeStruct((nb, l, BR), BF16),
                   jax.ShapeDtypeStruct((nb, 1, D_COLS), F32),
                   jax.ShapeDtypeStruct((nb, NHEAD, HEAD, HEAD), F32)],
        scratch_shapes=[pltpu.VMEM((tp + 8, D_COLS), F32), pltpu.VMEM((NHEAD // 2, 2 * HEAD, 2 * HEAD), F32)],
        compiler_params=_cparams("parallel", "arbitrary"),
    )(proj3, proj3, proj3, proj3, shift0, wkv0, *consts)


def _ds_prep_kernel(xr_ref, xk_ref, xv_ref, xg_ref, xwa_ref, sr_ref, sk_ref, sv_ref, sg_ref, swa_ref,
                    mr_ref, mk_ref, mv_ref, mg_ref, mwa_ref, w0_ref, a0_ref, kk_ref, ka_ref,
                    wl_ref, al_ref, gl_ref, blk_ref, *outs, nt):
    r_o, w_o, k_o, v_o, kk_o, bv_o, g_o = outs
    nb = sr_ref.shape[1]

    def mixed(x_ref, s_ref, m_ref, t):
        cur = x_ref[pl.ds(t, nb, stride=nt), :]
        prev = s_ref[0] if t == 0 else x_ref[pl.ds(t - 1, nb, stride=nt), :]
        return cur + (prev - cur) * m_ref[...]

    for t in range(nt):
        r = mixed(xr_ref, sr_ref, mr_ref, t)
        k = mixed(xk_ref, sk_ref, mk_ref, t)
        v = mixed(xv_ref, sv_ref, mv_ref, t)
        xg = mixed(xg_ref, sg_ref, mg_ref, t)
        xwa = mixed(xwa_ref, swa_ref, mwa_ref, t)
        lw = -jnp.exp(-_softplus(-(w0_ref[...] + _bmm(jnp.tanh(xwa), wl_ref[...]))) - 0.5)
        a = _sigmoid(a0_ref[...] + _bmm(xwa, al_ref[...]))
        kkr = k * kk_ref[...]
        kk = kkr / jnp.maximum(jnp.sqrt(_sel_mm(kkr * kkr, blk_ref[...], 3)), 1e-12)
        r_o[t] = r.T
        w_o[t] = jnp.exp(lw).T
        k_o[t] = (k * (1.0 + (a - 1.0) * ka_ref[...])).T
        v_o[t] = v.T
        kk_o[t] = kk.T
        bv_o[t] = (kk * a).T
        g_o[t] = _bmm(_sigmoid(xg), gl_ref[...]).T


def _ds_rec_kernel(s_ref, r_ref, w_ref, k_ref, v_ref, kk_ref, bv_ref, g_ref, rk_ref, gnw_ref, gnb_ref,
                   o_ref, sn_ref, y_scr, *, nt):
    for hh in range(2):
        c0 = hh * HEAD

        def row(i, carry):
            s = s_ref[0, hh, i]
            for t in range(nt):
                sa = jnp.sum(s * kk_ref[t, c0:c0 + HEAD, :], axis=0, keepdims=True)
                s = (s * w_ref[t, c0:c0 + HEAD, :] - sa * bv_ref[t, c0:c0 + HEAD, :]
                     + v_ref[t, pl.ds(c0 + i, 1), :] * k_ref[t, c0:c0 + HEAD, :])
                y_scr[t, pl.ds(c0 + i, 1), :] = jnp.sum(s * r_ref[t, c0:c0 + HEAD, :], axis=0, keepdims=True)
            sn_ref[hh, i] = s
            return carry

        lax.fori_loop(0, HEAD, row, 0)
    for t in range(nt):
        halves = []
        for hh in range(2):
            sl = slice(hh * HEAD, (hh + 1) * HEAD)
            y = y_scr[t, sl, :]
            mean = jnp.mean(y, axis=0, keepdims=True)
            d = y - mean
            var = jnp.mean(d * d, axis=0, keepdims=True)
            yn = d * lax.rsqrt(var + RWKV_GN_EPS) * gnw_ref[sl, :] + gnb_ref[sl, :]
            bonus = jnp.sum(r_ref[t, sl, :] * k_ref[t, sl, :] * rk_ref[sl, :], axis=0, keepdims=True)
            halves.append((yn + bonus * v_ref[t, sl, :]) * g_ref[t, sl, :])
        o_ref[t] = jnp.concatenate(halves, axis=0).T.astype(o_ref.dtype)


def _mixer_d_sample(proj, shift0, wkv_t, layer, prm, *, nt):
    m = proj.shape[0]
    nb = m // nt
    npair = NHEAD // 2
    w2 = 2 * HEAD

    def xcol(off):
        return pl.BlockSpec((m, w2), lambda p: (0, off // w2 + p))

    def scol(off):
        return pl.BlockSpec((1, nb, w2), lambda p: (layer, 0, off // w2 + p))

    def vcol(off):
        return pl.BlockSpec((1, w2), lambda p: (0, off // w2 + p))

    x_specs = [xcol(P_R), xcol(P_DK), xcol(P_DV),
               pl.BlockSpec((m, w2), lambda p: (0, P_DL // w2)), pl.BlockSpec((m, w2), lambda p: (0, P_DL // w2 + 1))]
    s_specs = [scol(0), scol(BR), scol(2 * BR),
               pl.BlockSpec((1, nb, w2), lambda p: (layer, 0, 3 * BR // w2)),
               pl.BlockSpec((1, nb, w2), lambda p: (layer, 0, 3 * BR // w2 + 1))]
    m_specs = [vcol(0), vcol(BR), vcol(2 * BR),
               pl.BlockSpec((1, w2), lambda p: (0, 3 * BR // w2)), pl.BlockSpec((1, w2), lambda p: (0, 3 * BR // w2 + 1))]
    p_specs = [vcol(0)] * 4
    w_specs = [pl.BlockSpec((w2, w2), lambda p: (0, p))] * 3
    tile = jax.ShapeDtypeStruct((nt, BR, nb), F32)
    tile_spec = pl.BlockSpec((nt, w2, nb), lambda p: (0, p, 0))
    tiles = pl.pallas_call(
        functools.partial(_ds_prep_kernel, nt=nt), grid=(npair,),
        in_specs=x_specs + s_specs + m_specs + p_specs + w_specs + [_const_spec(prm['ds_blk'], 1)],
        out_specs=[tile_spec] * 7, out_shape=[tile] * 7,
        compiler_params=_cparams("parallel"),
    )(*([proj] * 5), *([shift0] * 5), *([prm['d_mu']] * 5), prm['d_w0'], prm['d_a0'], prm['d_kk'], prm['d_ka'],
      prm['ds_wl'], prm['ds_al'], prm['ds_gl'], prm['ds_blk'])
    col_spec = pl.BlockSpec((w2, nb), lambda p: (p, 0))
    return pl.pallas_call(
        functools.partial(_ds_rec_kernel, nt=nt), grid=(npair,),
        in_specs=[pl.BlockSpec((1, 2, HEAD, HEAD, nb), lambda p: (layer, p, 0, 0, 0))] + [tile_spec] * 7
        + [col_spec] * 3,
        out_specs=[pl.BlockSpec((nt, nb, w2), lambda p: (0, 0, p)),
                   pl.BlockSpec((2, HEAD, HEAD, nb), lambda p: (p, 0, 0, 0))],
        out_shape=[jax.ShapeDtypeStruct((nt, nb, BR), BF16), jax.ShapeDtypeStruct((NHEAD, HEAD, HEAD, nb), F32)],
        scratch_shapes=[pltpu.VMEM((nt, w2, nb), F32)],
        compiler_params=_cparams("parallel"),
    )(wkv_t, *tiles, prm['ds_rk'], prm['ds_gnw'], prm['ds_gnb'])


_IN_SPLITS = (BR, BR, BR, BR, BR, B_BC, B_BC, NHEAD, BR, BR, D_LW, BR, BR, D_LA, D_LG, N_BRANCH * D_MODEL)
_IN_NAMES = ('q', 'k', 'v', 'z', 'x', 'bm', 'cm', 'dt', 'u', 'r', 'wl', 'dk', 'dv', 'al', 'gl', 'gate')
_PACK_ORDER = ('q', 'k', 'v', 'z', 'u', 'x', 'r', 'dk', 'dv', 'gl', 'wl', 'al', 'bm', 'cm', 'gate')
_D_SPLITS = (BR, D_LW, BR, BR, D_LA, D_LG)
_D_PERM = np.concatenate([np.arange(o, o + n) for o, n in (
    (0, BR), (BR + D_LW, BR), (2 * BR + D_LW, BR), (3 * BR + D_LW + D_LA, D_LG), (BR, D_LW), (3 * BR + D_LW, D_LA))])
_D_INV = np.argsort(_D_PERM)


def _head_block(width, group):
    idx = np.arange(width) // group
    return jnp.asarray((idx[:, None] == idx[None, :]).astype(np.float32))


def _layer_params(i, P, nb_sample):
    prm = {}
    offs = np.cumsum((0,) + _IN_SPLITS)
    wt = jnp.transpose(P['w_in'], (2, 0, 1))[:, i, :]
    cols = {n: wt[offs[j]:offs[j + 1]] for j, n in enumerate(_IN_NAMES)}
    prm['w_in'] = jnp.concatenate([cols[n] for n in _PACK_ORDER], axis=0).astype(BF16)
    prm['w_dt'] = jnp.pad(cols['dt'], ((0, 128 - NHEAD), (0, 0))).astype(BF16)
    prm['norm1'] = P['norm1'][i][None]
    prm['norm2'] = P['norm2'][i][None]
    prm['w_branch'] = P['w_branch'][i].astype(BF16)
    prm['w_out'] = P['w_out'][i].astype(BF16)
    prm['w_ff1'] = P['w_ff1'][i].astype(BF16)
    prm['w_ff2'] = P['w_ff2'][i].astype(BF16)
    prm['a_qg'] = jnp.tile(P['a_q_gain'][i], NHEAD)[None]
    prm['a_kg'] = jnp.tile(P['a_k_gain'][i], NHEAD)[None]
    prm['blk64'] = _head_block(BR, HEAD)
    pad12 = lambda v: jnp.pad(v, (0, 128 - NHEAD))[None]
    prm['b_cw'] = P['b_conv_w'][i]
    prm['b_cb'] = P['b_conv_b'][i][None]
    prm['b_dtb'] = pad12(P['b_dt_bias'][i])
    prm['b_a'] = pad12(-jnp.exp(P['b_a_log'][i]))
    prm['b_dsk'] = pad12(P['b_d'][i])
    prm['b_nw'] = P['b_norm'][i][None]
    prm['b_sel'] = jnp.asarray(np.eye(16, 128, dtype=np.float32))
    prm['b_gblk'] = _head_block(BR, BR // B_GROUPS) * (B_GROUPS / BR)
    a_re, a_im = P['c_a_re'][i], P['c_a_im'][i]
    step = jnp.exp(P['c_log_step'][i])[:, None]
    mag = jnp.exp(a_re * step)
    lb_re, lb_im = mag * jnp.cos(a_im * step), mag * jnp.sin(a_im * step)
    den = a_re * a_re + a_im * a_im
    f_re = ((lb_re - 1.0) * a_re + lb_im * a_im) / den
    f_im = (lb_im * a_re - (lb_re - 1.0) * a_im) / den
    b_re, b_im = P['c_b_re'][i], P['c_b_im'][i]
    bb_re = f_re[..., None] * b_re - f_im[..., None] * b_im
    bb_im = f_re[..., None] * b_im + f_im[..., None] * b_re
    eye_g = jnp.eye(C_NG, dtype=F32)
    bfull_re = jnp.einsum('gpc,gh->gchp', bb_re, eye_g).reshape(BR, C_NS)
    bfull_im = jnp.einsum('gpc,gh->gchp', bb_im, eye_g).reshape(BR, C_NS)
    cfull_re = jnp.einsum('gcp,gh->gphc', P['c_c_re'][i], eye_g).reshape(C_NS, BR)
    cfull_im = jnp.einsum('gcp,gh->gphc', P['c_c_im'][i], eye_g).reshape(C_NS, BR)
    prm['c_lbr'] = lb_re.reshape(1, C_NS)
    prm['c_lbi'] = lb_im.reshape(1, C_NS)
    prm['c_lb8'] = jnp.concatenate([lb_re.reshape(C_Q, C_LN), lb_im.reshape(C_Q, C_LN)], axis=1)
    prm['c_bf'] = jnp.concatenate([bfull_re, bfull_im], axis=1).astype(BF16)
    prm['c_cf'] = jnp.concatenate([cfull_re, -cfull_im], axis=0).astype(BF16)
    bq, cq = [], []
    for qq in range(C_Q):
        ws, s0 = _c_window(qq), qq * C_LN
        bq.append(jnp.concatenate([bfull_re[ws:ws + C_WIN, s0:s0 + C_LN], bfull_im[ws:ws + C_WIN, s0:s0 + C_LN]], 1))
        cq.append(jnp.concatenate([cfull_re[s0:s0 + C_LN, ws:ws + C_WIN], -cfull_im[s0:s0 + C_LN, ws:ws + C_WIN]], 0))
    prm['c_bq'] = jnp.stack(bq).astype(BF16)
    prm['c_cq'] = jnp.stack(cq).astype(BF16)
    prm['c_dsk'] = P['c_d'][i][None]
    prm['c_wg'] = P['c_w_glu'][i].astype(BF16)
    prm['c_bg'] = P['c_b_glu'][i][None]
    prm['d_mu'] = P['d_mu'][i][_D_PERM][None]
    for k_, n_ in (('d_w0', 'd_w0'), ('d_a0', 'd_a0'), ('d_kk', 'd_k_k'), ('d_ka', 'd_k_a'),
                   ('d_gnw', 'd_gn_w'), ('d_gnb', 'd_gn_b')):
        prm[k_] = P[n_][i][None]
    prm['d_rk'] = P['d_r_k'][i].reshape(1, BR)
    zl = lambda r: jnp.zeros((r, BR), F32)
    prm['d_gl'] = jnp.concatenate([P['d_g_lora'][i], zl(D_LW + D_LA)], 0).astype(BF16)
    prm['d_wl'] = jnp.concatenate([zl(D_LG), P['d_w_lora'][i], zl(D_LA)], 0).astype(BF16)
    prm['d_al'] = jnp.concatenate([zl(D_LG + D_LW), P['d_a_lora'][i]], 0).astype(BF16)
    prm['d_blk'] = prm['blk64'].astype(BF16)
    prm['ds_wl'] = jnp.concatenate([P['d_w_lora'][i], zl(D_LA)], 0).astype(BF16)
    prm['ds_al'] = jnp.concatenate([zl(D_LW), P['d_a_lora'][i]], 0).astype(BF16)
    prm['ds_gl'] = P['d_g_lora'][i].astype(BF16)
    prm['ds_blk'] = _head_block(2 * HEAD, HEAD).astype(BF16)
    for k_, n_ in (('ds_rk', 'd_r_k'), ('ds_gnw', 'd_gn_w'), ('ds_gnb', 'd_gn_b')):
        prm[k_] = jnp.broadcast_to(P[n_][i].reshape(BR, 1), (BR, nb_sample))
    return prm


def _layer(x, states, layer, prm, tabs, *, nb, prompt):
    m = x.shape[0]
    l = m // nb
    tm = min(m, 512)
    proj, dt = _norm_mm(x, prm['norm1'], prm['w_in'], prm['w_dt'], tm=tm, tn=1024)
    proj3 = proj.reshape(nb, l, P_COLS)
    dt3 = dt.reshape(nb, l, 128)
    qn, kv = _a_prep(proj, prm['a_qg'], prm['a_kg'], prm['d_blk'], tm=tm)
    if prompt:
        oa = _a_attn_prompt(qn, kv, tabs['a_bias'])
        kv_new = kv[m - min(A_WIN, m):].reshape(nb, min(A_WIN, m), 2, NHEAD, HEAD)
    else:
        oa = _a_attn_sample(qn, kv, states['kv'], layer, tabs['a_sample']).reshape(m, BR)
        kv_new = kv.reshape(nb, l, 2, NHEAD, HEAD)
    ob, conv_new, ssm_new = _mixer_b(proj3, dt3, states['conv'], states['ssm'], layer, prm, q=min(l, 128))
    if prompt:
        oc, hf = _mixer_c_prompt(proj, states['s5'], prm, t=min(l, 256))
        s5_new = jnp.stack([hf[:, :C_LN].reshape(C_NG, C_STATE), hf[:, C_LN:].reshape(C_NG, C_STATE)], -1)[None]
    else:
        u_t = proj3[:, :, P_U:P_U + BR].transpose(1, 0, 2)
        s0 = states['s5'][layer]
        oc_t, hr, hi = _mixer_c_sample(u_t, s0[..., 0].reshape(nb, C_NS), s0[..., 1].reshape(nb, C_NS), prm)
        oc = oc_t.transpose(1, 0, 2).reshape(m, BR)
        s5_new = jnp.stack([hr.reshape(nb, C_NG, C_STATE), hi.reshape(nb, C_NG, C_STATE)], -1)
    if prompt:
        od, sh_new, wkv_new = _mixer_d(proj3, states['shift'], states['wkv'], layer, prm, t=min(l, D_CHUNK))
        od = od.reshape(m, BR)
        shift_new = sh_new.reshape(nb, D_COLS)[:, _D_INV]
    else:
        od_t, wkv_t = _mixer_d_sample(proj, states['shift'], states['wkv_t'], layer, prm, nt=l)
        od = od_t.transpose(1, 0, 2).reshape(m, BR)
        shift_new = proj3[:, l - 1, P_R:P_R + D_COLS][:, _D_INV]
        wkv_new = wkv_t.transpose(3, 0, 1, 2)
    merged = _merge([oa, ob.reshape(m, BR), oc, od.reshape(m, BR)], prm['w_branch'], proj, tm=tm, tn=512)
    x = _mm_res(merged, prm['w_out'], x, tm=tm, tn=1024, tk=D_MODEL)
    hid = _norm_mm(x, prm['norm2'], prm['w_ff1'], tm=tm, tn=1024, relu2=True, out_dtype=BF16)
    x = _mm_res(hid, prm['w_ff2'], x, tm=tm, tn=1024, tk=2048)
    return x, (kv_new, conv_new, ssm_new, s5_new, shift_new, wkv_new)


def kernel(x_prompt, x_sample, cache_kv_a, state_conv, state_ssm, state_s5, state_shift, state_wkv,
           norm1, w_in, a_q_gain, a_k_gain,
           b_conv_w, b_conv_b, b_dt_bias, b_a_log, b_d, b_norm,
           c_a_re, c_a_im, c_log_step, c_b_re, c_b_im, c_c_re, c_c_im, c_d, c_w_glu, c_b_glu,
           d_mu, d_w0, d_w_lora, d_a0, d_a_lora, d_g_lora, d_k_k, d_k_a, d_r_k, d_gn_w, d_gn_b,
           w_branch, w_out, norm2, w_ff1, w_ff2):
    P = dict(norm1=norm1, w_in=w_in, a_q_gain=a_q_gain, a_k_gain=a_k_gain,
             b_conv_w=b_conv_w, b_conv_b=b_conv_b, b_dt_bias=b_dt_bias, b_a_log=b_a_log,
             b_d=b_d, b_norm=b_norm,
             c_a_re=c_a_re, c_a_im=c_a_im, c_log_step=c_log_step, c_b_re=c_b_re, c_b_im=c_b_im,
             c_c_re=c_c_re, c_c_im=c_c_im, c_d=c_d, c_w_glu=c_w_glu, c_b_glu=c_b_glu,
             d_mu=d_mu, d_w0=d_w0, d_w_lora=d_w_lora, d_a0=d_a0, d_a_lora=d_a_lora,
             d_g_lora=d_g_lora, d_k_k=d_k_k, d_k_a=d_k_a, d_r_k=d_r_k, d_gn_w=d_gn_w, d_gn_b=d_gn_b,
             w_branch=w_branch, w_out=w_out, norm2=norm2, w_ff1=w_ff1, w_ff2=w_ff2)
    depth = w_in.shape[0]
    bp, lp, _ = x_prompt.shape
    bs, ls, _ = x_sample.shape
    assert bp == 1 and ls == A_T
    tabs = {'a_bias': _a_band_bias(), 'a_sample': _a_sample_tables(cache_kv_a.shape[2])}
    zero_states = {'conv': jnp.zeros((1, bp, B_CONV - 1, B_CONV_DIM), F32),
                   'ssm': jnp.zeros((1, bp, NHEAD, HEAD, B_STATE), F32),
                   's5': jnp.zeros((C_Q, 2 * C_LN), F32),
                   'shift': jnp.zeros((1, bp, 1, D_COLS), F32),
                   'wkv': jnp.zeros((1, bp, NHEAD, HEAD, HEAD), F32)}
    samp_states = {'kv': jnp.transpose(cache_kv_a, (0, 1, 3, 4, 5, 2)), 'conv': state_conv, 'ssm': state_ssm, 's5': state_s5,
                   'shift': state_shift[:, :, _D_PERM], 'wkv_t': jnp.transpose(state_wkv, (0, 2, 3, 4, 1))}
    yp = x_prompt.reshape(bp * lp, D_MODEL)
    ys = x_sample.reshape(bs * ls, D_MODEL)
    new_p, new_s = [], []
    for i in range(depth):
        prm = _layer_params(i, P, bs)
        yp, st = _layer(yp, zero_states, 0, prm, tabs, nb=bp, prompt=True)
        new_p.append(st)
        ys, st = _layer(ys, samp_states, i, prm, tabs, nb=bs, prompt=False)
        new_s.append(st)
    outs_p = [jnp.stack(z) for z in zip(*new_p)]
    outs_s = [jnp.stack(z) for z in zip(*new_s)]
    res = [yp.reshape(bp, lp, D_MODEL), ys.reshape(bs, ls, D_MODEL)]
    for a, b in zip(outs_p, outs_s):
        res += [a, b]
    return tuple(res)
```

```python
import functools

import jax
import jax.numpy as jnp
import numpy as np
from jax import lax
from jax.experimental import pallas as pl
from jax.experimental.pallas import tpu as pltpu

F32 = jnp.float32
BF16 = jnp.bfloat16
HI = lax.Precision.HIGHEST

D_MODEL = 2048
BR = 768
N_BRANCH = 4
HEAD = 64
NHEAD = BR // HEAD
A_WIN = 2048
A_BLK = 128
A_NKB = A_WIN // A_BLK + 1
A_PATTERNS = ((128, 1), (512, 4), (2048, 16))
B_GROUPS = 4
B_STATE = 128
B_CONV = 4
B_BC = B_GROUPS * B_STATE
B_CONV_DIM = BR + 2 * B_BC
C_GROUP = 16
C_NG = BR // C_GROUP
C_STATE = 64
C_NS = C_NG * C_STATE
C_Q = 8
C_LN = C_NS // C_Q
C_WIN = 256
D_LW, D_LA, D_LG = 64, 64, 128
D_LORA = D_LW + D_LA + D_LG
D_COLS = 3 * BR + D_LORA
D_FF = 4 * D_MODEL
NORM_EPS = 1e-6
RWKV_GN_EPS = 64e-5
NEG = -1e30

P_Q, P_K, P_V, P_Z, P_U, P_X, P_R, P_DK, P_DV = (BR * n for n in range(9))
P_DL = 9 * BR
P_BM = P_DL + D_LORA
P_CM = P_BM + B_BC
P_GATE = P_CM + B_BC
P_COLS = P_GATE + N_BRANCH * D_MODEL
VMEM_LIMIT = 56 * 1024 * 1024
D_CHUNK = 64
D_PASSES = (3, 3, 3)


def _cparams(*sem):
    return pltpu.CompilerParams(dimension_semantics=sem, vmem_limit_bytes=VMEM_LIMIT)


def _mm(a, b, prec=None):
    return lax.dot_general(a, b, (((1,), (0,)), ((), ())), precision=prec, preferred_element_type=F32)


def _mm_nt(a, b, prec=None):
    return lax.dot_general(a, b, (((1,), (1,)), ((), ())), precision=prec, preferred_element_type=F32)


def _mm_tn(a, b, prec=None):
    return lax.dot_general(a, b, (((0,), (0,)), ((), ())), precision=prec, preferred_element_type=F32)


_DIMS = {'nn': (((1,), (0,)), ((), ())), 'nt': (((1,), (1,)), ((), ())), 'tn': (((0,), (0,)), ((), ()))}


def _split(x):
    hi = x.astype(BF16)
    return hi, (x - hi.astype(F32)).astype(BF16)


def _pdot(a, b, kind, passes):
    def dg(x, y, prec=None):
        return lax.dot_general(x, y, _DIMS[kind], precision=prec, preferred_element_type=F32)

    if passes == 6:
        return dg(a, b, HI)
    if passes == 1:
        return dg(a.astype(BF16), b.astype(BF16))
    ah, al = _split(a)
    bh, bl = _split(b)
    return dg(ah, bh) + (dg(ah, bl) + dg(al, bh))


def _sel_mm(x, sel_bf16, terms):
    acc = None
    for _ in range(terms):
        hi = x.astype(BF16)
        t = _mm(hi, sel_bf16)
        acc = t if acc is None else acc + t
        x = x - hi.astype(F32)
    return acc


def _head_sums(x, blk2_bf16, terms):
    w = blk2_bf16.shape[0]
    return jnp.concatenate([_sel_mm(x[:, j:j + w], blk2_bf16, terms) for j in range(0, x.shape[1], w)], axis=-1)


def _bmm(a, b):
    return _mm(a.astype(BF16), b.astype(BF16))


def _bmm_nt(a, b):
    return _mm_nt(a.astype(BF16), b.astype(BF16))


def _bmm_tn(a, b):
    return _mm_tn(a.astype(BF16), b.astype(BF16))


def _iota(shape, dim):
    return lax.broadcasted_iota(jnp.int32, shape, dim)


def _sigmoid(x):
    return 1.0 / (1.0 + jnp.exp(-x))


def _softplus(x):
    return jnp.maximum(x, 0.0) + jnp.log(1.0 + jnp.exp(-jnp.abs(x)))


def _silu(x):
    return x * _sigmoid(x)


def _const_spec(a, ngrid):
    nd = a.ndim
    return pl.BlockSpec(a.shape, lambda *_: (0,) * nd)


def _norm_mm_kernel(x_ref, g_ref, w_ref, *rest, relu2, with_dt):
    if with_dt:
        wdt_ref, o_ref, odt_ref, h_scr = rest
    else:
        o_ref, h_scr = rest
    mm = _mm_nt if with_dt else _mm

    @pl.when(pl.program_id(1) == 0)
    def _():
        x = x_ref[...]
        ms = jnp.mean(x * x, axis=-1, keepdims=True)
        h = (x * lax.rsqrt(ms + NORM_EPS) * g_ref[...]).astype(BF16)
        h_scr[...] = h
        if with_dt:
            odt_ref[...] = mm(h, wdt_ref[...])

    y = mm(h_scr[...], w_ref[...])
    if relu2:
        y = jnp.square(jnp.maximum(y, 0.0))
    o_ref[...] = y.astype(o_ref.dtype)


def _norm_mm(x, g, w, wdt=None, *, tm, tn, relu2=False, out_dtype=F32):
    m, k = x.shape
    with_dt = wdt is not None
    n = w.shape[0] if with_dt else w.shape[1]
    in_specs = [pl.BlockSpec((tm, k), lambda i, j: (i, 0)),
                pl.BlockSpec((1, k), lambda i, j: (0, 0)),
                pl.BlockSpec((tn, k), lambda i, j: (j, 0)) if with_dt else pl.BlockSpec((k, tn), lambda i, j: (0, j))]
    out_specs = pl.BlockSpec((tm, tn), lambda i, j: (i, j))
    out_shape = jax.ShapeDtypeStruct((m, n), out_dtype)
    args = [x, g, w]
    if with_dt:
        in_specs.append(pl.BlockSpec((128, k), lambda i, j: (0, 0)))
        out_specs = [out_specs, pl.BlockSpec((tm, 128), lambda i, j: (i, 0))]
        out_shape = [out_shape, jax.ShapeDtypeStruct((m, 128), F32)]
        args.append(wdt)
    return pl.pallas_call(
        functools.partial(_norm_mm_kernel, relu2=relu2, with_dt=with_dt),
        grid=(m // tm, n // tn), in_specs=in_specs, out_specs=out_specs, out_shape=out_shape,
        scratch_shapes=[pltpu.VMEM((tm, k), BF16)],
        compiler_params=_cparams("parallel", "arbitrary"),
    )(*args)


def _mm_res_kernel(a_ref, w_ref, r_ref, o_ref):
    @pl.when(pl.program_id(2) == 0)
    def _():
        o_ref[...] = r_ref[...]

    o_ref[...] += _mm(a_ref[...], w_ref[...])


def _mm_res(a, w, res, *, tm, tn, tk):
    m, k = a.shape
    n = w.shape[1]
    return pl.pallas_call(
        _mm_res_kernel, grid=(m // tm, n // tn, k // tk),
        in_specs=[pl.BlockSpec((tm, tk), lambda i, j, kk: (i, kk)),
                  pl.BlockSpec((tk, tn), lambda i, j, kk: (kk, j)),
                  pl.BlockSpec((tm, tn), lambda i, j, kk: (i, j))],
        out_specs=pl.BlockSpec((tm, tn), lambda i, j, kk: (i, j)),
        out_shape=jax.ShapeDtypeStruct((m, n), F32),
        compiler_params=_cparams("parallel", "parallel", "arbitrary"),
    )(a, w, res)


def _merge_kernel(oa, ob, oc, od, wb, g0, g1, g2, g3, o_ref):
    acc = None
    for n, (o, g) in enumerate(((oa, g0), (ob, g1), (oc, g2), (od, g3))):
        t = _sigmoid(g[...]) * _mm(o[...], wb[n])
        acc = t if acc is None else acc + t
    o_ref[...] = acc.astype(o_ref.dtype)


def _merge(outs, wb, proj, *, tm, tn):
    m = proj.shape[0]
    o_spec = pl.BlockSpec((tm, BR), lambda i, j: (i, 0))

    def g_spec(n):
        return pl.BlockSpec((tm, tn), lambda i, j: (i, (P_GATE + n * D_MODEL) // tn + j))

    return pl.pallas_call(
        _merge_kernel, grid=(m // tm, D_MODEL // tn),
        in_specs=[o_spec] * 4 + [pl.BlockSpec((N_BRANCH, BR, tn), lambda i, j: (0, 0, j))]
        + [g_spec(n) for n in range(N_BRANCH)],
        out_specs=pl.BlockSpec((tm, tn), lambda i, j: (i, j)),
        out_shape=jax.ShapeDtypeStruct((m, D_MODEL), BF16),
        compiler_params=_cparams("parallel", "parallel"),
    )(*outs, wb, proj, proj, proj, proj)


def _a_prep_kernel(q_ref, k_ref, v_ref, qg_ref, kg_ref, blk_ref, qn_ref, kv_ref):
    blk = blk_ref[...]
    q = q_ref[...]
    k = k_ref[...]
    qms = _sel_mm(q * q, blk, 3) * (1.0 / HEAD)
    kms = _sel_mm(k * k, blk, 3) * (1.0 / HEAD)
    qn_ref[...] = q * lax.rsqrt(qms + NORM_EPS) * qg_ref[...] * (HEAD ** -0.5)
    kv_ref[:, :BR] = k * lax.rsqrt(kms + NORM_EPS) * kg_ref[...]
    kv_ref[:, BR:] = v_ref[...]


def _a_prep(proj, qg, kg, blk, *, tm):
    m = proj.shape[0]

    def col(c):
        return pl.BlockSpec((tm, BR), lambda i: (i, c))

    vec = pl.BlockSpec((1, BR), lambda i: (0, 0))
    return pl.pallas_call(
        _a_prep_kernel, grid=(m // tm,),
        in_specs=[col(P_Q // BR), col(P_K // BR), col(P_V // BR), vec, vec, pl.BlockSpec((BR, BR), lambda i: (0, 0))],
        out_specs=[pl.BlockSpec((tm, BR), lambda i: (i, 0)), pl.BlockSpec((tm, 2 * BR), lambda i: (i, 0))],
        out_shape=[jax.ShapeDtypeStruct((m, BR), F32), jax.ShapeDtypeStruct((m, 2 * BR), F32)],
        compiler_params=_cparams("parallel"),
    )(proj, proj, proj, qg, kg, blk)


def _a_band_bias():
    a = np.arange(A_BLK)[:, None]
    c = np.arange(2 * A_BLK)[None, :]
    band = (c >= a) & (c <= a + A_BLK)
    return jnp.asarray(np.stack([np.where(band, 0.0, NEG), np.where(band & (c >= A_BLK), 0.0, NEG)]).astype(np.float32))


def _a_attn_kernel(q_ref, kp_ref, kc_ref, vp_ref, vc_ref, bias_ref, o_ref, k_scr, v_scr, m_scr, l_scr, acc_scr,
                   *, sb_len):
    sb = pl.program_id(1)
    k_scr[0:sb_len, :] = kp_ref[...]
    k_scr[sb_len:, :] = kc_ref[...]
    v_scr[0:sb_len, :] = vp_ref[...]
    v_scr[sb_len:, :] = vc_ref[...]
    lane = _iota((A_BLK, 2 * HEAD), 1)
    left = lane < HEAD
    first_sb = sb == 0
    for pi_, (w, d) in enumerate(A_PATTERNS):
        assert w // d == A_BLK and sb_len % (A_BLK * d) == 0
        for j in range(sb_len // (A_BLK * d)):
            for r in range(d):
                q0 = r + d * A_BLK * j
                k0 = sb_len + q0 - d * A_BLK
                rows_q = pl.ds(q0, A_BLK, stride=d) if d > 1 else pl.ds(q0, A_BLK)
                rows_k = pl.ds(k0, 2 * A_BLK, stride=d) if d > 1 else pl.ds(k0, 2 * A_BLK)
                q = q_ref[rows_q, :]
                kt = k_scr[rows_k, :].astype(BF16)
                vt = v_scr[rows_k, :].astype(BF16)
                if j == 0:
                    bias = jnp.where(first_sb, bias_ref[1], bias_ref[0])
                else:
                    bias = bias_ref[0]
                ms, ls, accs = [], [], []
                for h in range(2):
                    qh = jnp.where(left if h == 0 else ~left, q, 0.0).astype(BF16)
                    s = _mm_nt(qh, kt) + bias
                    mu = jnp.max(s, axis=-1, keepdims=True)
                    p = jnp.exp(s - mu)
                    ms.append(mu)
                    ls.append(jnp.sum(p, axis=-1, keepdims=True))
                    accs.append(_mm(p.astype(BF16), vt))
                mu = jnp.where(left, ms[0], ms[1])
                lu = jnp.where(left, ls[0], ls[1])
                au = jnp.where(left, accs[0], accs[1])
                if pi_ == 0:
                    m_scr[rows_q, :] = mu
                    l_scr[rows_q, :] = lu
                    acc_scr[rows_q, :] = au
                else:
                    mo = m_scr[rows_q, :]
                    mn = jnp.maximum(mo, mu)
                    eo = jnp.exp(mo - mn)
                    eu = jnp.exp(mu - mn)
                    m_scr[rows_q, :] = mn
                    l_scr[rows_q, :] = eo * l_scr[rows_q, :] + eu * lu
                    acc_scr[rows_q, :] = eo * acc_scr[rows_q, :] + eu * au
    o_ref[...] = (acc_scr[...] / l_scr[...]).astype(o_ref.dtype)


def _a_attn_prompt(qn, kv, bias):
    l = qn.shape[0]
    npair = NHEAD // 2
    sb_len = min(l, A_WIN)
    blk = (sb_len, 2 * HEAD)
    prev = lambda hp, s: (jnp.maximum(s - 1, 0), hp)
    prev_v = lambda hp, s: (jnp.maximum(s - 1, 0), npair + hp)
    return pl.pallas_call(
        functools.partial(_a_attn_kernel, sb_len=sb_len), grid=(npair, l // sb_len),
        in_specs=[pl.BlockSpec(blk, lambda hp, s: (s, hp)),
                  pl.BlockSpec(blk, prev), pl.BlockSpec(blk, lambda hp, s: (s, hp)),
                  pl.BlockSpec(blk, prev_v), pl.BlockSpec(blk, lambda hp, s: (s, npair + hp)),
                  _const_spec(bias, 2)],
        out_specs=pl.BlockSpec(blk, lambda hp, s: (s, hp)),
        out_shape=jax.ShapeDtypeStruct((l, BR), BF16),
        scratch_shapes=[pltpu.VMEM((2 * sb_len, 2 * HEAD), F32), pltpu.VMEM((2 * sb_len, 2 * HEAD), F32)]
        + [pltpu.VMEM(blk, F32)] * 3,
        compiler_params=_cparams("parallel", "arbitrary"),
    )(qn, kv, kv, kv, kv, bias)


A_T = 4


def _a_sample_tables(past):
    row_t = np.arange(64)[:, None] // 16
    d = past + row_t - np.arange(past)[None, :]
    mc = np.zeros(d.shape, np.float32)
    for w, dil in A_PATTERNS:
        mc += ((d >= 0) & (d <= w) & (d % dil == 0)).astype(np.float32)
    tk = np.arange(8)[None, :]
    mn = np.where((tk <= row_t) & (tk < A_T), 1.0 + (len(A_PATTERNS) - 1.0) * (tk == row_t), 0.0).astype(np.float32)
    hm = np.zeros((16, BR), np.float32)
    for h in range(NHEAD):
        hm[h, h * HEAD:(h + 1) * HEAD] = 1.0
    return jnp.asarray(mc), jnp.asarray(mn), jnp.asarray(hm)


def _a_sample_kernel(q_ref, kvn_ref, c_ref, hm_ref, mc_ref, mn_ref, o_ref, new_scr, *, past):
    hm = hm_ref[...]
    q = q_ref[0]
    qexp = jnp.concatenate([jnp.broadcast_to(q[t:t + 1, :], (16, BR)) * hm for t in range(A_T)], axis=0)
    qexp = qexp.astype(BF16)
    new_scr[...] = jnp.zeros_like(new_scr)
    new_scr[0:A_T, :] = kvn_ref[0]
    kt = c_ref[0, 0, 0].reshape(BR, past).astype(BF16)
    vt = c_ref[0, 0, 1].reshape(BR, past).astype(BF16)
    mc = mc_ref[...]
    mn = mn_ref[...]
    s = jnp.where(mc > 0.0, _mm(qexp, kt), NEG)
    sn = jnp.where(mn > 0.0, _mm_nt(qexp, new_scr[:, :BR].astype(BF16)), NEG)
    m = jnp.maximum(jnp.max(s, axis=-1, keepdims=True), jnp.max(sn, axis=-1, keepdims=True))
    p = mc * jnp.exp(s - m)
    pn = mn * jnp.exp(sn - m)
    den = jnp.sum(p, axis=-1, keepdims=True) + jnp.sum(pn, axis=-1, keepdims=True)
    r = (_mm_nt(p.astype(BF16), vt) + _mm(pn.astype(BF16), new_scr[:, BR:].astype(BF16))) / den
    for t in range(A_T):
        o_ref[0, t:t + 1, :] = jnp.sum(r[16 * t:16 * t + 16, :] * hm, axis=0, keepdims=True).astype(o_ref.dtype)


def _a_attn_sample(qn, kv_new, cache_t, layer, tables):
    nb, past = cache_t.shape[1], cache_t.shape[-1]
    mc, mn, hm = tables
    return pl.pallas_call(
        functools.partial(_a_sample_kernel, past=past), grid=(nb,),
        in_specs=[pl.BlockSpec((1, A_T, BR), lambda b: (b, 0, 0)),
                  pl.BlockSpec((1, A_T, 2 * BR), lambda b: (b, 0, 0)),
                  pl.BlockSpec((1, 1, 2, NHEAD, HEAD, past), lambda b: (layer, b, 0, 0, 0, 0)),
                  _const_spec(hm, 1), _const_spec(mc, 1), _const_spec(mn, 1)],
        out_specs=pl.BlockSpec((1, A_T, BR), lambda b: (b, 0, 0)),
        out_shape=jax.ShapeDtypeStruct((nb, A_T, BR), BF16),
        scratch_shapes=[pltpu.VMEM((8, 2 * BR), F32)],
        compiler_params=_cparams("parallel"),
    )(qn.reshape(nb, A_T, BR), kv_new.reshape(nb, A_T, 2 * BR), cache_t, hm, mc, mn)


def _b_kernel(z_ref, x_ref, bm_ref, cm_ref, dt_ref, conv0_ref, ssm0_ref, cw_ref, cb_ref, dtb_ref, a_ref, dsk_ref,
              nw_ref, sel_ref, gblk_ref, y_ref, conv_ref, ssm_ref, xp_scr, s_scr, pad_scr, *, q, qp):
    c = pl.program_id(1)

    @pl.when(c == 0)
    def _():
        xp_scr[...] = jnp.zeros_like(xp_scr)
        xp_scr[5:8, :] = conv0_ref[0, 0]
        s_scr[...] = ssm0_ref[0, 0]

    xp_scr[8:8 + q, 0:BR] = x_ref[0]
    xp_scr[8:8 + q, BR:BR + B_BC] = bm_ref[0]
    xp_scr[8:8 + q, BR + B_BC:] = cm_ref[0]
    cw = cw_ref[...]
    conv = (cb_ref[...] + cw[3:4] * xp_scr[8:8 + qp, :] + cw[2:3] * xp_scr[7:7 + qp, :]
            + cw[1:2] * xp_scr[6:6 + qp, :] + cw[0:1] * xp_scr[5:5 + qp, :])
    tail = xp_scr[5 + q:8 + q, :]
    xp_scr[5:8, :] = tail
    conv_ref[0] = tail
    u = _silu(conv)
    xs = u[:, :BR]
    dt = _softplus(dt_ref[0] + dtb_ref[...])
    z = z_ref[0]
    if q != qp:
        pad_scr[...] = jnp.zeros_like(pad_scr)
        pad_scr[0:q, :BR] = z
        pad_scr[0:q, BR:] = dt
        z = pad_scr[:, :BR]
        dt = pad_scr[:, BR:]
    adt = dt * a_ref[...]
    trib = _iota((qp, qp), 0) >= _iota((qp, qp), 1)
    acum = _mm(trib.astype(F32), adt, HI)
    acum_t = _mm_nt(sel_ref[...], acum, HI)
    alast = acum[qp - 1:qp, :]
    dend = jnp.exp(alast - acum)
    eacum = jnp.exp(acum)
    elast = jnp.exp(alast)
    ys = []
    hpg = NHEAD // B_GROUPS
    for g in range(B_GROUPS):
        bm = u[:, BR + g * B_STATE:BR + (g + 1) * B_STATE]
        cm = u[:, BR + B_BC + g * B_STATE:BR + B_BC + (g + 1) * B_STATE]
        gmat = _bmm_nt(cm, bm)
        for h in range(g * hpg, (g + 1) * hpg):
            xh = xs[:, h * HEAD:(h + 1) * HEAD]
            xdt = xh * dt[:, h:h + 1]
            lmat = jnp.where(trib, jnp.exp(acum[:, h:h + 1] - acum_t[h:h + 1, :]), 0.0)
            s_h = s_scr[h]
            ys.append(_bmm(gmat * lmat, xdt) + eacum[:, h:h + 1] * _bmm_nt(cm, s_h) + xh * dsk_ref[:, h:h + 1])
            s_scr[h] = elast[:, h:h + 1] * s_h + _bmm_tn(xdt * dend[:, h:h + 1], bm)
    y = jnp.concatenate(ys, axis=-1) * _silu(z)
    ms = _mm(y * y, gblk_ref[...], HI)
    y = y * lax.rsqrt(ms + NORM_EPS) * nw_ref[...]
    y_ref[0] = y[:q].astype(y_ref.dtype)
    ssm_ref[0] = s_scr[...]


def _mixer_b(proj3, dt3, conv0, ssm0, layer, prm, *, q):
    nb, l, _ = proj3.shape
    qp = max(q, 8)
    consts = [prm[k] for k in ('b_cw', 'b_cb', 'b_dtb', 'b_a', 'b_dsk', 'b_nw', 'b_sel', 'b_gblk')]
    return pl.pallas_call(
        functools.partial(_b_kernel, q=q, qp=qp), grid=(nb, l // q),
        in_specs=[pl.BlockSpec((1, q, BR), lambda b, c: (b, c, P_Z // BR)),
                  pl.BlockSpec((1, q, BR), lambda b, c: (b, c, P_X // BR)),
                  pl.BlockSpec((1, q, B_BC), lambda b, c: (b, c, P_BM // B_BC)),
                  pl.BlockSpec((1, q, B_BC), lambda b, c: (b, c, P_CM // B_BC)),
                  pl.BlockSpec((1, q, 128), lambda b, c: (b, c, 0)),
                  pl.BlockSpec((1, 1, B_CONV - 1, B_CONV_DIM), lambda b, c: (layer, b, 0, 0)),
                  pl.BlockSpec((1, 1, NHEAD, HEAD, B_STATE), lambda b, c: (layer, b, 0, 0, 0))]
        + [_const_spec(a, 2) for a in consts],
        out_specs=[pl.BlockSpec((1, q, BR), lambda b, c: (b, c, 0)),
                   pl.BlockSpec((1, B_CONV - 1, B_CONV_DIM), lambda b, c: (b, 0, 0)),
                   pl.BlockSpec((1, NHEAD, HEAD, B_STATE), lambda b, c: (b, 0, 0, 0))],
        out_shape=[jax.ShapeDtypeStruct((nb, l, BR), BF16),
                   jax.ShapeDtypeStruct((nb, B_CONV - 1, B_CONV_DIM), F32),
                   jax.ShapeDtypeStruct((nb, NHEAD, HEAD, B_STATE), F32)],
        scratch_shapes=[pltpu.VMEM((qp + 8, B_CONV_DIM), F32), pltpu.VMEM((NHEAD, HEAD, B_STATE), F32),
                        pltpu.VMEM((qp, BR + 128), F32)],
        compiler_params=_cparams("parallel", "arbitrary"),
    )(proj3, proj3, proj3, proj3, dt3, conv0, ssm0, *consts)


def _gelu_tanh(y):
    return 0.5 * y * (1.0 + jnp.tanh(0.7978845608028654 * (y + 0.044715 * (y * y * y))))


def _c_window(qq):
    first = qq * C_LN // C_STATE * C_GROUP
    return min(first // 128 * 128, BR - C_WIN)


def _c_prompt_kernel(u_ref, h0_ref, lb_ref, bq_ref, cq_ref, dsk_ref, wg_ref, bg_ref, o_ref, hf_ref,
                     bu_scr, hs_scr, h_scr, *, t):
    @pl.when(pl.program_id(0) == 0)
    def _():
        h_scr[...] = h0_ref[...]

    u = u_ref[...]
    ub = u.astype(BF16)
    ntile = 2 * C_LN // 128
    for qq in range(C_Q):
        ws = _c_window(qq)
        bu = _mm(ub[:, ws:ws + C_WIN], bq_ref[qq])
        for j in range(ntile):
            bu_scr[j, pl.ds(qq, t, stride=C_Q), :] = bu[:, j * 128:(j + 1) * 128]
    lb = [lb_ref[:, j * 128:(j + 1) * 128] for j in range(ntile)]
    half = ntile // 2

    def step(i, h):
        r0 = pl.multiple_of(i * C_Q, C_Q)
        new = []
        for j in range(half):
            new.append(lb[j] * h[j] - lb[half + j] * h[half + j] + bu_scr[j, pl.ds(r0, C_Q), :])
        for j in range(half):
            new.append(lb[j] * h[half + j] + lb[half + j] * h[j] + bu_scr[half + j, pl.ds(r0, C_Q), :])
        for j in range(ntile):
            hs_scr[j, pl.ds(r0, C_Q), :] = new[j]
        return tuple(new)

    h = lax.fori_loop(0, t, step, tuple(h_scr[:, j * 128:(j + 1) * 128] for j in range(ntile)), unroll=8)
    for j in range(ntile):
        h_scr[:, j * 128:(j + 1) * 128] = h[j]
    hf_ref[...] = h_scr[...]
    ytile = [None] * (BR // 128)
    for qq in range(C_Q):
        ws = _c_window(qq) // 128
        hq = jnp.concatenate([hs_scr[j, pl.ds(qq, t, stride=C_Q), :] for j in range(ntile)], axis=-1)
        yq = _mm(hq.astype(BF16), cq_ref[qq])
        for j in range(C_WIN // 128):
            part = yq[:, j * 128:(j + 1) * 128]
            ytile[ws + j] = part if ytile[ws + j] is None else ytile[ws + j] + part
    y = _gelu_tanh(jnp.concatenate(ytile, axis=-1) + dsk_ref[...] * u)
    o_ref[...] = (y * _sigmoid(_bmm(y, wg_ref[...]) + bg_ref[...])).astype(o_ref.dtype)


def _mixer_c_prompt(proj, h0, prm, *, t):
    l = proj.shape[0]
    consts = [prm[k] for k in ('c_lb8', 'c_bq', 'c_cq', 'c_dsk', 'c_wg', 'c_bg')]
    return pl.pallas_call(
        functools.partial(_c_prompt_kernel, t=t), grid=(l // t,),
        in_specs=[pl.BlockSpec((t, BR), lambda c: (c, P_U // BR)), _const_spec(h0, 1)]
        + [_const_spec(a, 1) for a in consts],
        out_specs=[pl.BlockSpec((t, BR), lambda c: (c, 0)), pl.BlockSpec((C_Q, 2 * C_LN), lambda c: (0, 0))],
        out_shape=[jax.ShapeDtypeStruct((l, BR), BF16), jax.ShapeDtypeStruct((C_Q, 2 * C_LN), F32)],
        scratch_shapes=[pltpu.VMEM((2 * C_LN // 128, C_Q * t, 128), F32),
                        pltpu.VMEM((2 * C_LN // 128, C_Q * t, 128), F32),
                        pltpu.VMEM((C_Q, 2 * C_LN), F32)],
        compiler_params=_cparams("arbitrary"),
    )(proj, h0, *consts)


def _c_sample_kernel(u_ref, hr0_ref, hi0_ref, lbr_ref, lbi_ref, bf_ref, cf_ref, dsk_ref, wg_ref, bg_ref,
                     o_ref, hr_ref, hi_ref, *, nt):
    hr = hr0_ref[...]
    hi = hi0_ref[...]
    lbr = lbr_ref[...]
    lbi = lbi_ref[...]
    for t in range(nt):
        u = u_ref[t]
        bu = _bmm(u, bf_ref[...])
        hr, hi = lbr * hr - lbi * hi + bu[:, :C_NS], lbr * hi + lbi * hr + bu[:, C_NS:]
        y = _bmm(hr, cf_ref[:C_NS, :]) + _bmm(hi, cf_ref[C_NS:, :]) + dsk_ref[...] * u
        y = _gelu_tanh(y)
        o_ref[t] = (y * _sigmoid(_bmm(y, wg_ref[...]) + bg_ref[...])).astype(o_ref.dtype)
    hr_ref[...] = hr
    hi_ref[...] = hi


def _mixer_c_sample(u_t, hr0, hi0, prm):
    nt, nb, _ = u_t.shape
    args = [u_t, hr0, hi0] + [prm[k] for k in ('c_lbr', 'c_lbi', 'c_bf', 'c_cf', 'c_dsk', 'c_wg', 'c_bg')]
    return pl.pallas_call(
        functools.partial(_c_sample_kernel, nt=nt), grid=(1,),
        in_specs=[_const_spec(a, 1) for a in args],
        out_specs=[pl.BlockSpec((nt, nb, BR), lambda c: (0, 0, 0)), pl.BlockSpec((nb, C_NS), lambda c: (0, 0)),
                   pl.BlockSpec((nb, C_NS), lambda c: (0, 0))],
        out_shape=[jax.ShapeDtypeStruct((nt, nb, BR), BF16), jax.ShapeDtypeStruct((nb, C_NS), F32),
                   jax.ShapeDtypeStruct((nb, C_NS), F32)],
        compiler_params=_cparams("arbitrary"),
    )(*args)


def _d_kernel(r_ref, k_ref, v_ref, l_ref, sh0_ref, wkv0_ref, mu_ref, w0_ref, a0_ref, kk_ref, ka_ref, rk_ref,
              gnw_ref, gnb_ref, wl_ref, al_ref, gl_ref, blk_ref, y_ref, sh_ref, wkv_ref, x_scr, ht_scr, *, t, tp,
              pa, pi, ps):
    c = pl.program_id(1)

    @pl.when(c == 0)
    def _():
        x_scr[...] = jnp.zeros_like(x_scr)
        x_scr[7:8, :] = sh0_ref[0, 0]
        zero = jnp.zeros((HEAD, HEAD), F32)
        for p in range(NHEAD // 2):
            ht_scr[p, 0:HEAD, :] = jnp.concatenate([wkv0_ref[0, 0, 2 * p], zero], axis=-1)
            ht_scr[p, HEAD:, :] = jnp.concatenate([zero, wkv0_ref[0, 0, 2 * p + 1]], axis=-1)

    x_scr[8:8 + t, 0:BR] = r_ref[0]
    x_scr[8:8 + t, BR:2 * BR] = k_ref[0]
    x_scr[8:8 + t, 2 * BR:3 * BR] = v_ref[0]
    x_scr[8:8 + t, 3 * BR:] = l_ref[0]
    cur = x_scr[8:8 + tp, :]
    xm = cur + (x_scr[7:7 + tp, :] - cur) * mu_ref[...]
    last = x_scr[7 + t:8 + t, :]
    x_scr[7:8, :] = last
    sh_ref[0] = last
    r = xm[:, :BR]
    k = xm[:, BR:2 * BR]
    v = xm[:, 2 * BR:3 * BR]
    xl = xm[:, 3 * BR:]
    lw = -jnp.exp(-_softplus(-(w0_ref[...] + _bmm(jnp.tanh(xl), wl_ref[...]))) - 0.5)
    a = _sigmoid(a0_ref[...] + _bmm(xl, al_ref[...]))
    g = _bmm(_sigmoid(xl), gl_ref[...])
    blk = blk_ref[...]
    kkr = k * kk_ref[...]
    kk = kkr / jnp.maximum(jnp.sqrt(_head_sums(kkr * kkr, blk, 3)), 1e-12)
    k2 = k * (1.0 + (a - 1.0) * ka_ref[...])
    if t != tp:
        live = (_iota((tp, 1), 0) < t).astype(F32)
        r, k2, v, kk, lw = r * live, k2 * live, v * live, kk * live, lw * live
    bv = kk * a
    row = _iota((tp, tp), 0)
    col = _iota((tp, tp), 1)
    cum = _pdot((row >= col).astype(F32), lw, 'nn', 6)
    cend = cum[tp - 1:tp, :]
    cmid = cum[tp // 2 - 1:tp // 2, :]
    e_in = jnp.exp(cum - cmid)
    e_neg = jnp.exp(cmid - cum)
    e_end = jnp.exp(cend - cum)
    rd = r * jnp.exp(cum)
    kkd = kk * jnp.exp(cum - lw)
    rd_c = r * e_in
    kkd_c = kk * jnp.exp(cum - lw - cmid)
    ks = k2 * e_neg
    bs = bv * e_neg
    kse = k2 * e_end
    bse = bv * e_end
    dend = jnp.exp(cend)
    tp2 = 2 * tp
    row2 = _iota((tp2, tp2), 0) % tp
    col2 = _iota((tp2, tp2), 1) % tp
    incl = row2 >= col2
    strict = row2 > col2
    eye = (_iota((tp2, tp2), 0) == _iota((tp2, tp2), 1)).astype(F32)
    left = _iota((tp, 2 * HEAD), 1) < HEAD
    base = min(16, tp)
    blocks = []
    sz = base
    while sz <= tp:
        blocks.append((row2 // sz) == (col2 // sz))
        sz *= 2

    def stack(x, p):
        xp = x[:, 2 * HEAD * p:2 * HEAD * (p + 1)]
        return jnp.concatenate([jnp.where(left, xp, 0.0), jnp.where(left, 0.0, xp)], axis=0)

    ys = []
    for p in range(NHEAD // 2):
        kkd_s, bs_s, ks_s, rd_s, v_s = stack(kkd, p), stack(bs, p), stack(ks, p), stack(rd, p), stack(v, p)
        kkd_cs, rd_cs = stack(kkd_c, p), stack(rd_c, p)
        akb = jnp.where(strict, _pdot(kkd_cs, bs_s, 'nt', pa), 0.0)
        akk = jnp.where(strict, _pdot(kkd_cs, ks_s, 'nt', pa), 0.0)
        arb = jnp.where(incl, _pdot(rd_cs, bs_s, 'nt', pa), 0.0)
        ark = jnp.where(incl, _pdot(rd_cs, ks_s, 'nt', pa), 0.0)
        nb_ = jnp.where(blocks[0], akb, 0.0)
        inv = eye - nb_
        pw = _pdot(nb_, nb_, 'nn', pi)
        n = 2
        while n < base:
            inv = inv + _pdot(inv, pw, 'nn', pi)
            n *= 2
            if n < base:
                pw = _pdot(pw, pw, 'nn', pi)
        for lvl in range(1, len(blocks)):
            off = jnp.where(blocks[lvl] & ~blocks[lvl - 1], akb, 0.0)
            inv = inv - _pdot(inv, _pdot(off, inv, 'nn', pi), 'nn', pi)
        w1 = _pdot(inv, kkd_s, 'nn', pi)
        w2 = _pdot(inv, _pdot(akk, v_s, 'nn', pa), 'nn', pi)
        ht = ht_scr[p]
        u = _pdot(w1, ht, 'nt', ps) + w2
        y_s = _pdot(rd_s, ht, 'nt', ps) + _pdot(ark, v_s, 'nn', pa) - _pdot(arb, u, 'nn', pa)
        ys.append(y_s[:tp] + y_s[tp:])
        ht_scr[p] = (ht * dend[:, 2 * HEAD * p:2 * HEAD * (p + 1)] + _pdot(v_s, stack(kse, p), 'tn', ps)
                     - _pdot(u, stack(bse, p), 'tn', ps))
    y = jnp.concatenate(ys, axis=-1)
    mean = _head_sums(y, blk, 3) * (1.0 / HEAD)
    d = y - mean
    var = _head_sums(d * d, blk, 3) * (1.0 / HEAD)
    yn = d * lax.rsqrt(var + RWKV_GN_EPS) * gnw_ref[...] + gnb_ref[...]
    bonus = _head_sums(r * k2 * rk_ref[...], blk, 3)
    y_ref[0] = (((yn + bonus * v) * g)[:t]).astype(y_ref.dtype)

    @pl.when(c == pl.num_programs(1) - 1)
    def _():
        for p in range(NHEAD // 2):
            wkv_ref[0, 2 * p] = ht_scr[p, 0:HEAD, 0:HEAD]
            wkv_ref[0, 2 * p + 1] = ht_scr[p, HEAD:, HEAD:]


def _mixer_d(proj3, shift0, wkv0, layer, prm, *, t):
    nb, l, _ = proj3.shape
    tp = max(t, 8)
    consts = [prm[k] for k in ('d_mu', 'd_w0', 'd_a0', 'd_kk', 'd_ka', 'd_rk', 'd_gnw', 'd_gnb',
                               'd_wl', 'd_al', 'd_gl', 'ds_blk')]
    return pl.pallas_call(
        functools.partial(_d_kernel, t=t, tp=tp, pa=D_PASSES[0], pi=D_PASSES[1], ps=D_PASSES[2]),
        grid=(nb, l // t),
        in_specs=[pl.BlockSpec((1, t, BR), lambda b, c: (b, c, P_R // BR)),
                  pl.BlockSpec((1, t, BR), lambda b, c: (b, c, P_DK // BR)),
                  pl.BlockSpec((1, t, BR), lambda b, c: (b, c, P_DV // BR)),
                  pl.BlockSpec((1, t, D_LORA), lambda b, c: (b, c, P_DL // D_LORA)),
                  pl.BlockSpec((1, 1, 1, D_COLS), lambda b, c: (layer, b, 0, 0)),
                  pl.BlockSpec((1, 1, NHEAD, HEAD, HEAD), lambda b, c: (layer, b, 0, 0, 0))]
        + [_const_spec(a, 2) for a in consts],
        out_specs=[pl.BlockSpec((1, t, BR), lambda b, c: (b, c, 0)),
                   pl.BlockSpec((1, 1, D_COLS), lambda b, c: (b, 0, 0)),
                   pl.BlockSpec((1, NHEAD, HEAD, HEAD), lambda b, c: (b, 0, 0, 0))],
        out_shape=[jax.ShapeDtypeStruct((nb, l, BR), BF16),
                   jax.ShapeDtypeStruct((nb, 1, D_COLS), F32),
                   jax.ShapeDtypeStruct((nb, NHEAD, HEAD, HEAD), F32)],
        scratch_shapes=[pltpu.VMEM((tp + 8, D_COLS), F32), pltpu.VMEM((NHEAD // 2, 2 * HEAD, 2 * HEAD), F32)],
        compiler_params=_cparams("parallel", "arbitrary"),
    )(proj3, proj3, proj3, proj3, shift0, wkv0, *consts)


def _ds_prep_kernel(xr_ref, xk_ref, xv_ref, xg_ref, xwa_ref, sr_ref, sk_ref, sv_ref, sg_ref, swa_ref,
                    mr_ref, mk_ref, mv_ref, mg_ref, mwa_ref, w0_ref, a0_ref, kk_ref, ka_ref,
                    wl_ref, al_ref, gl_ref, blk_ref, *outs, nt):
    r_o, w_o, k_o, v_o, kk_o, bv_o, g_o = outs
    nb = sr_ref.shape[1]

    def mixed(x_ref, s_ref, m_ref, t):
        cur = x_ref[pl.ds(t, nb, stride=nt), :]
        prev = s_ref[0] if t == 0 else x_ref[pl.ds(t - 1, nb, stride=nt), :]
        return cur + (prev - cur) * m_ref[...]

    for t in range(nt):
        r = mixed(xr_ref, sr_ref, mr_ref, t)
        k = mixed(xk_ref, sk_ref, mk_ref, t)
        v = mixed(xv_ref, sv_ref, mv_ref, t)
        xg = mixed(xg_ref, sg_ref, mg_ref, t)
        xwa = mixed(xwa_ref, swa_ref, mwa_ref, t)
        lw = -jnp.exp(-_softplus(-(w0_ref[...] + _bmm(jnp.tanh(xwa), wl_ref[...]))) - 0.5)
        a = _sigmoid(a0_ref[...] + _bmm(xwa, al_ref[...]))
        kkr = k * kk_ref[...]
        kk = kkr / jnp.maximum(jnp.sqrt(_sel_mm(kkr * kkr, blk_ref[...], 3)), 1e-12)
        r_o[t] = r.T
        w_o[t] = jnp.exp(lw).T
        k_o[t] = (k * (1.0 + (a - 1.0) * ka_ref[...])).T
        v_o[t] = v.T
        kk_o[t] = kk.T
        bv_o[t] = (kk * a).T
        g_o[t] = _bmm(_sigmoid(xg), gl_ref[...]).T


def _ds_rec_kernel(s_ref, r_ref, w_ref, k_ref, v_ref, kk_ref, bv_ref, g_ref, rk_ref, gnw_ref, gnb_ref,
                   o_ref, sn_ref, y_scr, *, nt):
    for hh in range(2):
        c0 = hh * HEAD

        def row(i, carry):
            s = s_ref[0, hh, i]
            for t in range(nt):
                sa = jnp.sum(s * kk_ref[t, c0:c0 + HEAD, :], axis=0, keepdims=True)
                s = (s * w_ref[t, c0:c0 + HEAD, :] - sa * bv_ref[t, c0:c0 + HEAD, :]
                     + v_ref[t, pl.ds(c0 + i, 1), :] * k_ref[t, c0:c0 + HEAD, :])
                y_scr[t, pl.ds(c0 + i, 1), :] = jnp.sum(s * r_ref[t, c0:c0 + HEAD, :], axis=0, keepdims=True)
            sn_ref[hh, i] = s
            return carry

        lax.fori_loop(0, HEAD, row, 0)
    for t in range(nt):
        halves = []
        for hh in range(2):
            sl = slice(hh * HEAD, (hh + 1) * HEAD)
            y = y_scr[t, sl, :]
            mean = jnp.mean(y, axis=0, keepdims=True)
            d = y - mean
            var = jnp.mean(d * d, axis=0, keepdims=True)
            yn = d * lax.rsqrt(var + RWKV_GN_EPS) * gnw_ref[sl, :] + gnb_ref[sl, :]
            bonus = jnp.sum(r_ref[t, sl, :] * k_ref[t, sl, :] * rk_ref[sl, :], axis=0, keepdims=True)
            halves.append((yn + bonus * v_ref[t, sl, :]) * g_ref[t, sl, :])
        o_ref[t] = jnp.concatenate(halves, axis=0).T.astype(o_ref.dtype)


def _mixer_d_sample(proj, shift0, wkv_t, layer, prm, *, nt):
    m = proj.shape[0]
    nb = m // nt
    npair = NHEAD // 2
    w2 = 2 * HEAD

    def xcol(off):
        return pl.BlockSpec((m, w2), lambda p: (0, off // w2 + p))

    def scol(off):
        return pl.BlockSpec((1, nb, w2), lambda p: (layer, 0, off // w2 + p))

    def vcol(off):
        return pl.BlockSpec((1, w2), lambda p: (0, off // w2 + p))

    x_specs = [xcol(P_R), xcol(P_DK), xcol(P_DV),
               pl.BlockSpec((m, w2), lambda p: (0, P_DL // w2)), pl.BlockSpec((m, w2), lambda p: (0, P_DL // w2 + 1))]
    s_specs = [scol(0), scol(BR), scol(2 * BR),
               pl.BlockSpec((1, nb, w2), lambda p: (layer, 0, 3 * BR // w2)),
               pl.BlockSpec((1, nb, w2), lambda p: (layer, 0, 3 * BR // w2 + 1))]
    m_specs = [vcol(0), vcol(BR), vcol(2 * BR),
               pl.BlockSpec((1, w2), lambda p: (0, 3 * BR // w2)), pl.BlockSpec((1, w2), lambda p: (0, 3 * BR // w2 + 1))]
    p_specs = [vcol(0)] * 4
    w_specs = [pl.BlockSpec((w2, w2), lambda p: (0, p))] * 3
    tile = jax.ShapeDtypeStruct((nt, BR, nb), F32)
    tile_spec = pl.BlockSpec((nt, w2, nb), lambda p: (0, p, 0))
    tiles = pl.pallas_call(
        functools.partial(_ds_prep_kernel, nt=nt), grid=(npair,),
        in_specs=x_specs + s_specs + m_specs + p_specs + w_specs + [_const_spec(prm['ds_blk'], 1)],
        out_specs=[tile_spec] * 7, out_shape=[tile] * 7,
        compiler_params=_cparams("parallel"),
    )(*([proj] * 5), *([shift0] * 5), *([prm['d_mu']] * 5), prm['d_w0'], prm['d_a0'], prm['d_kk'], prm['d_ka'],
      prm['ds_wl'], prm['ds_al'], prm['ds_gl'], prm['ds_blk'])
    col_spec = pl.BlockSpec((w2, nb), lambda p: (p, 0))
    return pl.pallas_call(
        functools.partial(_ds_rec_kernel, nt=nt), grid=(npair,),
        in_specs=[pl.BlockSpec((1, 2, HEAD, HEAD, nb), lambda p: (layer, p, 0, 0, 0))] + [tile_spec] * 7
        + [col_spec] * 3,
        out_specs=[pl.BlockSpec((nt, nb, w2), lambda p: (0, 0, p)),
                   pl.BlockSpec((2, HEAD, HEAD, nb), lambda p: (p, 0, 0, 0))],
        out_shape=[jax.ShapeDtypeStruct((nt, nb, BR), BF16), jax.ShapeDtypeStruct((NHEAD, HEAD, HEAD, nb), F32)],
        scratch_shapes=[pltpu.VMEM((nt, w2, nb), F32)],
        compiler_params=_cparams("parallel"),
    )(wkv_t, *tiles, prm['ds_rk'], prm['ds_gnw'], prm['ds_gnb'])


_IN_SPLITS = (BR, BR, BR, BR, BR, B_BC, B_BC, NHEAD, BR, BR, D_LW, BR, BR, D_LA, D_LG, N_BRANCH * D_MODEL)
_IN_NAMES = ('q', 'k', 'v', 'z', 'x', 'bm', 'cm', 'dt', 'u', 'r', 'wl', 'dk', 'dv', 'al', 'gl', 'gate')
_PACK_ORDER = ('q', 'k', 'v', 'z', 'u', 'x', 'r', 'dk', 'dv', 'gl', 'wl', 'al', 'bm', 'cm', 'gate')
_D_SPLITS = (BR, D_LW, BR, BR, D_LA, D_LG)
_D_PERM = np.concatenate([np.arange(o, o + n) for o, n in (
    (0, BR), (BR + D_LW, BR), (2 * BR + D_LW, BR), (3 * BR + D_LW + D_LA, D_LG), (BR, D_LW), (3 * BR + D_LW, D_LA))])
_D_INV = np.argsort(_D_PERM)


def _head_block(width, group):
    idx = np.arange(width) // group
    return jnp.asarray((idx[:, None] == idx[None, :]).astype(np.float32))


def _layer_params(i, P, nb_sample):
    prm = {}
    offs = np.cumsum((0,) + _IN_SPLITS)
    wt = jnp.transpose(P['w_in'], (2, 0, 1))[:, i, :]
    cols = {n: wt[offs[j]:offs[j + 1]] for j, n in enumerate(_IN_NAMES)}
    prm['w_in'] = jnp.concatenate([cols[n] for n in _PACK_ORDER], axis=0).astype(BF16)
    prm['w_dt'] = jnp.pad(cols['dt'], ((0, 128 - NHEAD), (0, 0))).astype(BF16)
    prm['norm1'] = P['norm1'][i][None]
    prm['norm2'] = P['norm2'][i][None]
    prm['w_branch'] = P['w_branch'][i].astype(BF16)
    prm['w_out'] = P['w_out'][i].astype(BF16)
    prm['w_ff1'] = P['w_ff1'][i].astype(BF16)
    prm['w_ff2'] = P['w_ff2'][i].astype(BF16)
    prm['a_qg'] = jnp.tile(P['a_q_gain'][i], NHEAD)[None]
    prm['a_kg'] = jnp.tile(P['a_k_gain'][i], NHEAD)[None]
    prm['blk64'] = _head_block(BR, HEAD)
    pad12 = lambda v: jnp.pad(v, (0, 128 - NHEAD))[None]
    prm['b_cw'] = P['b_conv_w'][i]
    prm['b_cb'] = P['b_conv_b'][i][None]
    prm['b_dtb'] = pad12(P['b_dt_bias'][i])
    prm['b_a'] = pad12(-jnp.exp(P['b_a_log'][i]))
    prm['b_dsk'] = pad12(P['b_d'][i])
    prm['b_nw'] = P['b_norm'][i][None]
    prm['b_sel'] = jnp.asarray(np.eye(16, 128, dtype=np.float32))
    prm['b_gblk'] = _head_block(BR, BR // B_GROUPS) * (B_GROUPS / BR)
    a_re, a_im = P['c_a_re'][i], P['c_a_im'][i]
    step = jnp.exp(P['c_log_step'][i])[:, None]
    mag = jnp.exp(a_re * step)
    lb_re, lb_im = mag * jnp.cos(a_im * step), mag * jnp.sin(a_im * step)
    den = a_re * a_re + a_im * a_im
    f_re = ((lb_re - 1.0) * a_re + lb_im * a_im) / den
    f_im = (lb_im * a_re - (lb_re - 1.0) * a_im) / den
    b_re, b_im = P['c_b_re'][i], P['c_b_im'][i]
    bb_re = f_re[..., None] * b_re - f_im[..., None] * b_im
    bb_im = f_re[..., None] * b_im + f_im[..., None] * b_re
    eye_g = jnp.eye(C_NG, dtype=F32)
    bfull_re = jnp.einsum('gpc,gh->gchp', bb_re, eye_g).reshape(BR, C_NS)
    bfull_im = jnp.einsum('gpc,gh->gchp', bb_im, eye_g).reshape(BR, C_NS)
    cfull_re = jnp.einsum('gcp,gh->gphc', P['c_c_re'][i], eye_g).reshape(C_NS, BR)
    cfull_im = jnp.einsum('gcp,gh->gphc', P['c_c_im'][i], eye_g).reshape(C_NS, BR)
    prm['c_lbr'] = lb_re.reshape(1, C_NS)
    prm['c_lbi'] = lb_im.reshape(1, C_NS)
    prm['c_lb8'] = jnp.concatenate([lb_re.reshape(C_Q, C_LN), lb_im.reshape(C_Q, C_LN)], axis=1)
    prm['c_bf'] = jnp.concatenate([bfull_re, bfull_im], axis=1).astype(BF16)
    prm['c_cf'] = jnp.concatenate([cfull_re, -cfull_im], axis=0).astype(BF16)
    bq, cq = [], []
    for qq in range(C_Q):
        ws, s0 = _c_window(qq), qq * C_LN
        bq.append(jnp.concatenate([bfull_re[ws:ws + C_WIN, s0:s0 + C_LN], bfull_im[ws:ws + C_WIN, s0:s0 + C_LN]], 1))
        cq.append(jnp.concatenate([cfull_re[s0:s0 + C_LN, ws:ws + C_WIN], -cfull_im[s0:s0 + C_LN, ws:ws + C_WIN]], 0))
    prm['c_bq'] = jnp.stack(bq).astype(BF16)
    prm['c_cq'] = jnp.stack(cq).astype(BF16)
    prm['c_dsk'] = P['c_d'][i][None]
    prm['c_wg'] = P['c_w_glu'][i].astype(BF16)
    prm['c_bg'] = P['c_b_glu'][i][None]
    prm['d_mu'] = P['d_mu'][i][_D_PERM][None]
    for k_, n_ in (('d_w0', 'd_w0'), ('d_a0', 'd_a0'), ('d_kk', 'd_k_k'), ('d_ka', 'd_k_a'),
                   ('d_gnw', 'd_gn_w'), ('d_gnb', 'd_gn_b')):
        prm[k_] = P[n_][i][None]
    prm['d_rk'] = P['d_r_k'][i].reshape(1, BR)
    zl = lambda r: jnp.zeros((r, BR), F32)
    prm['d_gl'] = jnp.concatenate([P['d_g_lora'][i], zl(D_LW + D_LA)], 0).astype(BF16)
    prm['d_wl'] = jnp.concatenate([zl(D_LG), P['d_w_lora'][i], zl(D_LA)], 0).astype(BF16)
    prm['d_al'] = jnp.concatenate([zl(D_LG + D_LW), P['d_a_lora'][i]], 0).astype(BF16)
    prm['d_blk'] = prm['blk64'].astype(BF16)
    prm['ds_wl'] = jnp.concatenate([P['d_w_lora'][i], zl(D_LA)], 0).astype(BF16)
    prm['ds_al'] = jnp.concatenate([zl(D_LW), P['d_a_lora'][i]], 0).astype(BF16)
    prm['ds_gl'] = P['d_g_lora'][i].astype(BF16)
    prm['ds_blk'] = _head_block(2 * HEAD, HEAD).astype(BF16)
    for k_, n_ in (('ds_rk', 'd_r_k'), ('ds_gnw', 'd_gn_w'), ('ds_gnb', 'd_gn_b')):
        prm[k_] = jnp.broadcast_to(P[n_][i].reshape(BR, 1), (BR, nb_sample))
    return prm


def _layer(x, states, layer, prm, tabs, *, nb, prompt):
    m = x.shape[0]
    l = m // nb
    tm = min(m, 512)
    proj, dt = _norm_mm(x, prm['norm1'], prm['w_in'], prm['w_dt'], tm=tm, tn=1024)
    proj3 = proj.reshape(nb, l, P_COLS)
    dt3 = dt.reshape(nb, l, 128)
    qn, kv = _a_prep(proj, prm['a_qg'], prm['a_kg'], prm['d_blk'], tm=tm)
    if prompt:
        oa = _a_attn_prompt(qn, kv, tabs['a_bias'])
        kv_new = kv[m - min(A_WIN, m):].reshape(nb, min(A_WIN, m), 2, NHEAD, HEAD)
    else:
        oa = _a_attn_sample(qn, kv, states['kv'], layer, tabs['a_sample']).reshape(m, BR)
        kv_new = kv.reshape(nb, l, 2, NHEAD, HEAD)
    ob, conv_new, ssm_new = _mixer_b(proj3, dt3, states['conv'], states['ssm'], layer, prm, q=min(l, 128))
    if prompt:
        oc, hf = _mixer_c_prompt(proj, states['s5'], prm, t=min(l, 256))
        s5_new = jnp.stack([hf[:, :C_LN].reshape(C_NG, C_STATE), hf[:, C_LN:].reshape(C_NG, C_STATE)], -1)[None]
    else:
        u_t = proj3[:, :, P_U:P_U + BR].transpose(1, 0, 2)
        s0 = states['s5'][layer]
        oc_t, hr, hi = _mixer_c_sample(u_t, s0[..., 0].reshape(nb, C_NS), s0[..., 1].reshape(nb, C_NS), prm)
        oc = oc_t.transpose(1, 0, 2).reshape(m, BR)
        s5_new = jnp.stack([hr.reshape(nb, C_NG, C_STATE), hi.reshape(nb, C_NG, C_STATE)], -1)
    if prompt:
        od, sh_new, wkv_new = _mixer_d(proj3, states['shift'], states['wkv'], layer, prm, t=min(l, D_CHUNK))
        od = od.reshape(m, BR)
        shift_new = sh_new.reshape(nb, D_COLS)[:, _D_INV]
    else:
        od_t, wkv_t = _mixer_d_sample(proj, states['shift'], states['wkv_t'], layer, prm, nt=l)
        od = od_t.transpose(1, 0, 2).reshape(m, BR)
        shift_new = proj3[:, l - 1, P_R:P_R + D_COLS][:, _D_INV]
        wkv_new = wkv_t.transpose(3, 0, 1, 2)
    merged = _merge([oa, ob.reshape(m, BR), oc, od.reshape(m, BR)], prm['w_branch'], proj, tm=tm, tn=512)
    x = _mm_res(merged, prm['w_out'], x, tm=tm, tn=1024, tk=D_MODEL)
    hid = _norm_mm(x, prm['norm2'], prm['w_ff1'], tm=tm, tn=1024, relu2=True, out_dtype=BF16)
    x = _mm_res(hid, prm['w_ff2'], x, tm=tm, tn=1024, tk=2048)
    return x, (kv_new, conv_new, ssm_new, s5_new, shift_new, wkv_new)


def kernel(x_prompt, x_sample, cache_kv_a, state_conv, state_ssm, state_s5, state_shift, state_wkv,
           norm1, w_in, a_q_gain, a_k_gain,
           b_conv_w, b_conv_b, b_dt_bias, b_a_log, b_d, b_norm,
           c_a_re, c_a_im, c_log_step, c_b_re, c_b_im, c_c_re, c_c_im, c_d, c_w_glu, c_b_glu,
           d_mu, d_w0, d_w_lora, d_a0, d_a_lora, d_g_lora, d_k_k, d_k_a, d_r_k, d_gn_w, d_gn_b,
           w_branch, w_out, norm2, w_ff1, w_ff2):
    P = dict(norm1=norm1, w_in=w_in, a_q_gain=a_q_gain, a_k_gain=a_k_gain,
             b_conv_w=b_conv_w, b_conv_b=b_conv_b, b_dt_bias=b_dt_bias, b_a_log=b_a_log,
             b_d=b_d, b_norm=b_norm,
             c_a_re=c_a_re, c_a_im=c_a_im, c_log_step=c_log_step, c_b_re=c_b_re, c_b_im=c_b_im,
             c_c_re=c_c_re, c_c_im=c_c_im, c_d=c_d, c_w_glu=c_w_glu, c_b_glu=c_b_glu,
             d_mu=d_mu, d_w0=d_w0, d_w_lora=d_w_lora, d_a0=d_a0, d_a_lora=d_a_lora,
             d_g_lora=d_g_lora, d_k_k=d_k_k, d_k_a=d_k_a, d_r_k=d_r_k, d_gn_w=d_gn_w, d_gn_b=d_gn_b,
             w_branch=w_branch, w_out=w_out, norm2=norm2, w_ff1=w_ff1, w_ff2=w_ff2)
    depth = w_in.shape[0]
    bp, lp, _ = x_prompt.shape
    bs, ls, _ = x_sample.shape
    assert bp == 1 and ls == A_T
    tabs = {'a_bias': _a_band_bias(), 'a_sample': _a_sample_tables(cache_kv_a.shape[2])}
    zero_states = {'conv': jnp.zeros((1, bp, B_CONV - 1, B_CONV_DIM), F32),
                   'ssm': jnp.zeros((1, bp, NHEAD, HEAD, B_STATE), F32),
                   's5': jnp.zeros((C_Q, 2 * C_LN), F32),
                   'shift': jnp.zeros((1, bp, 1, D_COLS), F32),
                   'wkv': jnp.zeros((1, bp, NHEAD, HEAD, HEAD), F32)}
    samp_states = {'kv': jnp.transpose(cache_kv_a, (0, 1, 3, 4, 5, 2)), 'conv': state_conv, 'ssm': state_ssm, 's5': state_s5,
                   'shift': state_shift[:, :, _D_PERM], 'wkv_t': jnp.transpose(state_wkv, (0, 2, 3, 4, 1))}
    yp = x_prompt.reshape(bp * lp, D_MODEL)
    ys = x_sample.reshape(bs * ls, D_MODEL)
    new_p, new_s = [], []
    for i in range(depth):
        prm = _layer_params(i, P, bs)
        yp, st = _layer(yp, zero_states, 0, prm, tabs, nb=bp, prompt=True)
        new_p.append(st)
        ys, st = _layer(ys, samp_states, i, prm, tabs, nb=bs, prompt=False)
        new_s.append(st)
    outs_p = [jnp.stack(z) for z in zip(*new_p)]
    outs_s = [jnp.stack(z) for z in zip(*new_s)]
    res = [yp.reshape(bp, lp, D_MODEL), ys.reshape(bs, ls, D_MODEL)]
    for a, b in zip(outs_p, outs_s):
        res += [a, b]
    return tuple(res)
```

```python
import functools

import jax
import jax.numpy as jnp
import numpy as np
from jax import lax
from jax.experimental import pallas as pl
from jax.experimental.pallas import tpu as pltpu

F32 = jnp.float32
BF16 = jnp.bfloat16
HI = lax.Precision.HIGHEST

D_MODEL = 2048
BR = 768
N_BRANCH = 4
HEAD = 64
NHEAD = BR // HEAD
A_WIN = 2048
A_BLK = 128
A_NKB = A_WIN // A_BLK + 1
A_PATTERNS = ((128, 1), (512, 4), (2048, 16))
B_GROUPS = 4
B_STATE = 128
B_CONV = 4
B_BC = B_GROUPS * B_STATE
B_CONV_DIM = BR + 2 * B_BC
C_GROUP = 16
C_NG = BR // C_GROUP
C_STATE = 64
C_NS = C_NG * C_STATE
C_Q = 8
C_LN = C_NS // C_Q
C_WIN = 256
D_LW, D_LA, D_LG = 64, 64, 128
D_LORA = D_LW + D_LA + D_LG
D_COLS = 3 * BR + D_LORA
D_FF = 4 * D_MODEL
NORM_EPS = 1e-6
RWKV_GN_EPS = 64e-5
NEG = -1e30

P_Q, P_K, P_V, P_Z, P_U, P_X, P_R, P_DK, P_DV = (BR * n for n in range(9))
P_DL = 9 * BR
P_BM = P_DL + D_LORA
P_CM = P_BM + B_BC
P_GATE = P_CM + B_BC
P_COLS = P_GATE + N_BRANCH * D_MODEL
VMEM_LIMIT = 56 * 1024 * 1024
D_CHUNK = 128
D_PASSES = (3, 3, 3)


def _cparams(*sem):
    return pltpu.CompilerParams(dimension_semantics=sem, vmem_limit_bytes=VMEM_LIMIT)


def _mm(a, b, prec=None):
    return lax.dot_general(a, b, (((1,), (0,)), ((), ())), precision=prec, preferred_element_type=F32)


def _mm_nt(a, b, prec=None):
    return lax.dot_general(a, b, (((1,), (1,)), ((), ())), precision=prec, preferred_element_type=F32)


def _mm_tn(a, b, prec=None):
    return lax.dot_general(a, b, (((0,), (0,)), ((), ())), precision=prec, preferred_element_type=F32)


_DIMS = {'nn': (((1,), (0,)), ((), ())), 'nt': (((1,), (1,)), ((), ())), 'tn': (((0,), (0,)), ((), ()))}


def _split(x):
    hi = x.astype(BF16)
    return hi, (x - hi.astype(F32)).astype(BF16)


def _pdot(a, b, kind, passes):
    def dg(x, y, prec=None):
        return lax.dot_general(x, y, _DIMS[kind], precision=prec, preferred_element_type=F32)

    if passes == 6:
        return dg(a, b, HI)
    if passes == 1:
        return dg(a.astype(BF16), b.astype(BF16))
    ah, al = _split(a)
    bh, bl = _split(b)
    return dg(ah, bh) + (dg(ah, bl) + dg(al, bh))


def _sel_mm(x, sel_bf16, terms):
    acc = None
    for _ in range(terms):
        hi = x.astype(BF16)
        t = _mm(hi, sel_bf16)
        acc = t if acc is None else acc + t
        x = x - hi.astype(F32)
    return acc


def _head_sums(x, blk2_bf16, terms):
    w = blk2_bf16.shape[0]
    return jnp.concatenate([_sel_mm(x[:, j:j + w], blk2_bf16, terms) for j in range(0, x.shape[1], w)], axis=-1)


def _bmm(a, b):
    return _mm(a.astype(BF16), b.astype(BF16))


def _bmm_nt(a, b):
    return _mm_nt(a.astype(BF16), b.astype(BF16))


def _bmm_tn(a, b):
    return _mm_tn(a.astype(BF16), b.astype(BF16))


def _iota(shape, dim):
    return lax.broadcasted_iota(jnp.int32, shape, dim)


def _sigmoid(x):
    return 1.0 / (1.0 + jnp.exp(-x))


def _softplus(x):
    return jnp.maximum(x, 0.0) + jnp.log(1.0 + jnp.exp(-jnp.abs(x)))


def _silu(x):
    return x * _sigmoid(x)


def _const_spec(a, ngrid):
    nd = a.ndim
    return pl.BlockSpec(a.shape, lambda *_: (0,) * nd)


def _norm_mm_kernel(x_ref, g_ref, w_ref, *rest, relu2, with_dt):
    if with_dt:
        wdt_ref, o_ref, odt_ref, h_scr = rest
    else:
        o_ref, h_scr = rest
    mm = _mm_nt if with_dt else _mm

    @pl.when(pl.program_id(1) == 0)
    def _():
        x = x_ref[...]
        ms = jnp.mean(x * x, axis=-1, keepdims=True)
        h = (x * lax.rsqrt(ms + NORM_EPS) * g_ref[...]).astype(BF16)
        h_scr[...] = h
        if with_dt:
            odt_ref[...] = mm(h, wdt_ref[...])

    y = mm(h_scr[...], w_ref[...])
    if relu2:
        y = jnp.square(jnp.maximum(y, 0.0))
    o_ref[...] = y.astype(o_ref.dtype)


def _norm_mm(x, g, w, wdt=None, *, tm, tn, relu2=False, out_dtype=F32):
    m, k = x.shape
    with_dt = wdt is not None
    n = w.shape[0] if with_dt else w.shape[1]
    in_specs = [pl.BlockSpec((tm, k), lambda i, j: (i, 0)),
                pl.BlockSpec((1, k), lambda i, j: (0, 0)),
                pl.BlockSpec((tn, k), lambda i, j: (j, 0)) if with_dt else pl.BlockSpec((k, tn), lambda i, j: (0, j))]
    out_specs = pl.BlockSpec((tm, tn), lambda i, j: (i, j))
    out_shape = jax.ShapeDtypeStruct((m, n), out_dtype)
    args = [x, g, w]
    if with_dt:
        in_specs.append(pl.BlockSpec((128, k), lambda i, j: (0, 0)))
        out_specs = [out_specs, pl.BlockSpec((tm, 128), lambda i, j: (i, 0))]
        out_shape = [out_shape, jax.ShapeDtypeStruct((m, 128), F32)]
        args.append(wdt)
    return pl.pallas_call(
        functools.partial(_norm_mm_kernel, relu2=relu2, with_dt=with_dt),
        grid=(m // tm, n // tn), in_specs=in_specs, out_specs=out_specs, out_shape=out_shape,
        scratch_shapes=[pltpu.VMEM((tm, k), BF16)],
        compiler_params=_cparams("parallel", "arbitrary"),
    )(*args)


def _mm_res_kernel(a_ref, w_ref, r_ref, o_ref):
    @pl.when(pl.program_id(2) == 0)
    def _():
        o_ref[...] = r_ref[...]

    o_ref[...] += _mm(a_ref[...], w_ref[...])


def _mm_res(a, w, res, *, tm, tn, tk):
    m, k = a.shape
    n = w.shape[1]
    return pl.pallas_call(
        _mm_res_kernel, grid=(m // tm, n // tn, k // tk),
        in_specs=[pl.BlockSpec((tm, tk), lambda i, j, kk: (i, kk)),
                  pl.BlockSpec((tk, tn), lambda i, j, kk: (kk, j)),
                  pl.BlockSpec((tm, tn), lambda i, j, kk: (i, j))],
        out_specs=pl.BlockSpec((tm, tn), lambda i, j, kk: (i, j)),
        out_shape=jax.ShapeDtypeStruct((m, n), F32),
        compiler_params=_cparams("parallel", "parallel", "arbitrary"),
    )(a, w, res)


def _merge_kernel(oa, ob, oc, od, wb, g0, g1, g2, g3, o_ref):
    acc = None
    for n, (o, g) in enumerate(((oa, g0), (ob, g1), (oc, g2), (od, g3))):
        t = _sigmoid(g[...]) * _mm(o[...], wb[n])
        acc = t if acc is None else acc + t
    o_ref[...] = acc.astype(o_ref.dtype)


def _merge(outs, wb, proj, *, tm, tn):
    m = proj.shape[0]
    o_spec = pl.BlockSpec((tm, BR), lambda i, j: (i, 0))

    def g_spec(n):
        return pl.BlockSpec((tm, tn), lambda i, j: (i, (P_GATE + n * D_MODEL) // tn + j))

    return pl.pallas_call(
        _merge_kernel, grid=(m // tm, D_MODEL // tn),
        in_specs=[o_spec] * 4 + [pl.BlockSpec((N_BRANCH, BR, tn), lambda i, j: (0, 0, j))]
        + [g_spec(n) for n in range(N_BRANCH)],
        out_specs=pl.BlockSpec((tm, tn), lambda i, j: (i, j)),
        out_shape=jax.ShapeDtypeStruct((m, D_MODEL), BF16),
        compiler_params=_cparams("parallel", "parallel"),
    )(*outs, wb, proj, proj, proj, proj)


def _a_prep_kernel(q_ref, k_ref, v_ref, qg_ref, kg_ref, blk_ref, qn_ref, kv_ref):
    blk = blk_ref[...]
    q = q_ref[...]
    k = k_ref[...]
    qms = _sel_mm(q * q, blk, 3) * (1.0 / HEAD)
    kms = _sel_mm(k * k, blk, 3) * (1.0 / HEAD)
    qn_ref[...] = q * lax.rsqrt(qms + NORM_EPS) * qg_ref[...] * (HEAD ** -0.5)
    kv_ref[:, :BR] = k * lax.rsqrt(kms + NORM_EPS) * kg_ref[...]
    kv_ref[:, BR:] = v_ref[...]


def _a_prep(proj, qg, kg, blk, *, tm):
    m = proj.shape[0]

    def col(c):
        return pl.BlockSpec((tm, BR), lambda i: (i, c))

    vec = pl.BlockSpec((1, BR), lambda i: (0, 0))
    return pl.pallas_call(
        _a_prep_kernel, grid=(m // tm,),
        in_specs=[col(P_Q // BR), col(P_K // BR), col(P_V // BR), vec, vec, pl.BlockSpec((BR, BR), lambda i: (0, 0))],
        out_specs=[pl.BlockSpec((tm, BR), lambda i: (i, 0)), pl.BlockSpec((tm, 2 * BR), lambda i: (i, 0))],
        out_shape=[jax.ShapeDtypeStruct((m, BR), F32), jax.ShapeDtypeStruct((m, 2 * BR), F32)],
        compiler_params=_cparams("parallel"),
    )(proj, proj, proj, qg, kg, blk)


def _a_band_bias():
    a = np.arange(A_BLK)[:, None]
    c = np.arange(2 * A_BLK)[None, :]
    band = (c >= a) & (c <= a + A_BLK)
    return jnp.asarray(np.stack([np.where(band, 0.0, NEG), np.where(band & (c >= A_BLK), 0.0, NEG)]).astype(np.float32))


def _a_attn_kernel(q_ref, kp_ref, kc_ref, vp_ref, vc_ref, bias_ref, o_ref, k_scr, v_scr, m_scr, l_scr, acc_scr,
                   *, sb_len):
    sb = pl.program_id(1)
    k_scr[0:sb_len, :] = kp_ref[...]
    k_scr[sb_len:, :] = kc_ref[...]
    v_scr[0:sb_len, :] = vp_ref[...]
    v_scr[sb_len:, :] = vc_ref[...]
    lane = _iota((A_BLK, 2 * HEAD), 1)
    left = lane < HEAD
    first_sb = sb == 0
    for pi_, (w, d) in enumerate(A_PATTERNS):
        assert w // d == A_BLK and sb_len % (A_BLK * d) == 0
        for j in range(sb_len // (A_BLK * d)):
            for r in range(d):
                q0 = r + d * A_BLK * j
                k0 = sb_len + q0 - d * A_BLK
                rows_q = pl.ds(q0, A_BLK, stride=d) if d > 1 else pl.ds(q0, A_BLK)
                rows_k = pl.ds(k0, 2 * A_BLK, stride=d) if d > 1 else pl.ds(k0, 2 * A_BLK)
                q = q_ref[rows_q, :]
                kt = k_scr[rows_k, :].astype(BF16)
                vt = v_scr[rows_k, :].astype(BF16)
                if j == 0:
                    bias = jnp.where(first_sb, bias_ref[1], bias_ref[0])
                else:
                    bias = bias_ref[0]
                ms, ls, accs = [], [], []
                for h in range(2):
                    qh = jnp.where(left if h == 0 else ~left, q, 0.0).astype(BF16)
                    s = _mm_nt(qh, kt) + bias
                    mu = jnp.max(s, axis=-1, keepdims=True)
                    p = jnp.exp(s - mu)
                    ms.append(mu)
                    ls.append(jnp.sum(p, axis=-1, keepdims=True))
                    accs.append(_mm(p.astype(BF16), vt))
                mu = jnp.where(left, ms[0], ms[1])
                lu = jnp.where(left, ls[0], ls[1])
                au = jnp.where(left, accs[0], accs[1])
                if pi_ == 0:
                    m_scr[rows_q, :] = mu
                    l_scr[rows_q, :] = lu
                    acc_scr[rows_q, :] = au
                else:
                    mo = m_scr[rows_q, :]
                    mn = jnp.maximum(mo, mu)
                    eo = jnp.exp(mo - mn)
                    eu = jnp.exp(mu - mn)
                    m_scr[rows_q, :] = mn
                    l_scr[rows_q, :] = eo * l_scr[rows_q, :] + eu * lu
                    acc_scr[rows_q, :] = eo * acc_scr[rows_q, :] + eu * au
    o_ref[...] = (acc_scr[...] / l_scr[...]).astype(o_ref.dtype)


def _a_attn_prompt(qn, kv, bias):
    l = qn.shape[0]
    npair = NHEAD // 2
    sb_len = min(l, A_WIN)
    blk = (sb_len, 2 * HEAD)
    prev = lambda hp, s: (jnp.maximum(s - 1, 0), hp)
    prev_v = lambda hp, s: (jnp.maximum(s - 1, 0), npair + hp)
    return pl.pallas_call(
        functools.partial(_a_attn_kernel, sb_len=sb_len), grid=(npair, l // sb_len),
        in_specs=[pl.BlockSpec(blk, lambda hp, s: (s, hp)),
                  pl.BlockSpec(blk, prev), pl.BlockSpec(blk, lambda hp, s: (s, hp)),
                  pl.BlockSpec(blk, prev_v), pl.BlockSpec(blk, lambda hp, s: (s, npair + hp)),
                  _const_spec(bias, 2)],
        out_specs=pl.BlockSpec(blk, lambda hp, s: (s, hp)),
        out_shape=jax.ShapeDtypeStruct((l, BR), BF16),
        scratch_shapes=[pltpu.VMEM((2 * sb_len, 2 * HEAD), F32), pltpu.VMEM((2 * sb_len, 2 * HEAD), F32)]
        + [pltpu.VMEM(blk, F32)] * 3,
        compiler_params=_cparams("parallel", "arbitrary"),
    )(qn, kv, kv, kv, kv, bias)


A_T = 4


def _a_sample_tables(past):
    row_t = np.arange(64)[:, None] // 16
    d = past + row_t - np.arange(past)[None, :]
    mc = np.zeros(d.shape, np.float32)
    for w, dil in A_PATTERNS:
        mc += ((d >= 0) & (d <= w) & (d % dil == 0)).astype(np.float32)
    tk = np.arange(8)[None, :]
    mn = np.where((tk <= row_t) & (tk < A_T), 1.0 + (len(A_PATTERNS) - 1.0) * (tk == row_t), 0.0).astype(np.float32)
    hm = np.zeros((16, BR), np.float32)
    for h in range(NHEAD):
        hm[h, h * HEAD:(h + 1) * HEAD] = 1.0
    return jnp.asarray(mc), jnp.asarray(mn), jnp.asarray(hm)


def _a_sample_kernel(q_ref, kvn_ref, c_ref, hm_ref, mc_ref, mn_ref, o_ref, new_scr, *, past):
    hm = hm_ref[...]
    q = q_ref[0]
    qexp = jnp.concatenate([jnp.broadcast_to(q[t:t + 1, :], (16, BR)) * hm for t in range(A_T)], axis=0)
    qexp = qexp.astype(BF16)
    new_scr[...] = jnp.zeros_like(new_scr)
    new_scr[0:A_T, :] = kvn_ref[0]
    kt = c_ref[0, 0, 0].reshape(BR, past).astype(BF16)
    vt = c_ref[0, 0, 1].reshape(BR, past).astype(BF16)
    mc = mc_ref[...]
    mn = mn_ref[...]
    s = jnp.where(mc > 0.0, _mm(qexp, kt), NEG)
    sn = jnp.where(mn > 0.0, _mm_nt(qexp, new_scr[:, :BR].astype(BF16)), NEG)
    m = jnp.maximum(jnp.max(s, axis=-1, keepdims=True), jnp.max(sn, axis=-1, keepdims=True))
    p = mc * jnp.exp(s - m)
    pn = mn * jnp.exp(sn - m)
    den = jnp.sum(p, axis=-1, keepdims=True) + jnp.sum(pn, axis=-1, keepdims=True)
    r = (_mm_nt(p.astype(BF16), vt) + _mm(pn.astype(BF16), new_scr[:, BR:].astype(BF16))) / den
    for t in range(A_T):
        o_ref[0, t:t + 1, :] = jnp.sum(r[16 * t:16 * t + 16, :] * hm, axis=0, keepdims=True).astype(o_ref.dtype)


def _a_attn_sample(qn, kv_new, cache_t, layer, tables):
    nb, past = cache_t.shape[1], cache_t.shape[-1]
    mc, mn, hm = tables
    return pl.pallas_call(
        functools.partial(_a_sample_kernel, past=past), grid=(nb,),
        in_specs=[pl.BlockSpec((1, A_T, BR), lambda b: (b, 0, 0)),
                  pl.BlockSpec((1, A_T, 2 * BR), lambda b: (b, 0, 0)),
                  pl.BlockSpec((1, 1, 2, NHEAD, HEAD, past), lambda b: (layer, b, 0, 0, 0, 0)),
                  _const_spec(hm, 1), _const_spec(mc, 1), _const_spec(mn, 1)],
        out_specs=pl.BlockSpec((1, A_T, BR), lambda b: (b, 0, 0)),
        out_shape=jax.ShapeDtypeStruct((nb, A_T, BR), BF16),
        scratch_shapes=[pltpu.VMEM((8, 2 * BR), F32)],
        compiler_params=_cparams("parallel"),
    )(qn.reshape(nb, A_T, BR), kv_new.reshape(nb, A_T, 2 * BR), cache_t, hm, mc, mn)


def _b_kernel(z_ref, x_ref, bm_ref, cm_ref, dt_ref, conv0_ref, ssm0_ref, cw_ref, cb_ref, dtb_ref, a_ref, dsk_ref,
              nw_ref, sel_ref, gblk_ref, y_ref, conv_ref, ssm_ref, xp_scr, s_scr, pad_scr, *, q, qp):
    c = pl.program_id(1)

    @pl.when(c == 0)
    def _():
        xp_scr[...] = jnp.zeros_like(xp_scr)
        xp_scr[5:8, :] = conv0_ref[0, 0]
        s_scr[...] = ssm0_ref[0, 0]

    xp_scr[8:8 + q, 0:BR] = x_ref[0]
    xp_scr[8:8 + q, BR:BR + B_BC] = bm_ref[0]
    xp_scr[8:8 + q, BR + B_BC:] = cm_ref[0]
    cw = cw_ref[...]
    conv = (cb_ref[...] + cw[3:4] * xp_scr[8:8 + qp, :] + cw[2:3] * xp_scr[7:7 + qp, :]
            + cw[1:2] * xp_scr[6:6 + qp, :] + cw[0:1] * xp_scr[5:5 + qp, :])
    tail = xp_scr[5 + q:8 + q, :]
    xp_scr[5:8, :] = tail
    conv_ref[0] = tail
    u = _silu(conv)
    xs = u[:, :BR]
    dt = _softplus(dt_ref[0] + dtb_ref[...])
    z = z_ref[0]
    if q != qp:
        pad_scr[...] = jnp.zeros_like(pad_scr)
        pad_scr[0:q, :BR] = z
        pad_scr[0:q, BR:] = dt
        z = pad_scr[:, :BR]
        dt = pad_scr[:, BR:]
    adt = dt * a_ref[...]
    trib = _iota((qp, qp), 0) >= _iota((qp, qp), 1)
    acum = _mm(trib.astype(F32), adt, HI)
    acum_t = _mm_nt(sel_ref[...], acum, HI)
    alast = acum[qp - 1:qp, :]
    dend = jnp.exp(alast - acum)
    eacum = jnp.exp(acum)
    elast = jnp.exp(alast)
    ys = []
    hpg = NHEAD // B_GROUPS
    for g in range(B_GROUPS):
        bm = u[:, BR + g * B_STATE:BR + (g + 1) * B_STATE]
        cm = u[:, BR + B_BC + g * B_STATE:BR + B_BC + (g + 1) * B_STATE]
        gmat = _bmm_nt(cm, bm)
        for h in range(g * hpg, (g + 1) * hpg):
            xh = xs[:, h * HEAD:(h + 1) * HEAD]
            xdt = xh * dt[:, h:h + 1]
            lmat = jnp.where(trib, jnp.exp(acum[:, h:h + 1] - acum_t[h:h + 1, :]), 0.0)
            s_h = s_scr[h]
            ys.append(_bmm(gmat * lmat, xdt) + eacum[:, h:h + 1] * _bmm_nt(cm, s_h) + xh * dsk_ref[:, h:h + 1])
            s_scr[h] = elast[:, h:h + 1] * s_h + _bmm_tn(xdt * dend[:, h:h + 1], bm)
    y = jnp.concatenate(ys, axis=-1) * _silu(z)
    ms = _mm(y * y, gblk_ref[...], HI)
    y = y * lax.rsqrt(ms + NORM_EPS) * nw_ref[...]
    y_ref[0] = y[:q].astype(y_ref.dtype)
    ssm_ref[0] = s_scr[...]


def _mixer_b(proj3, dt3, conv0, ssm0, layer, prm, *, q):
    nb, l, _ = proj3.shape
    qp = max(q, 8)
    consts = [prm[k] for k in ('b_cw', 'b_cb', 'b_dtb', 'b_a', 'b_dsk', 'b_nw', 'b_sel', 'b_gblk')]
    return pl.pallas_call(
        functools.partial(_b_kernel, q=q, qp=qp), grid=(nb, l // q),
        in_specs=[pl.BlockSpec((1, q, BR), lambda b, c: (b, c, P_Z // BR)),
                  pl.BlockSpec((1, q, BR), lambda b, c: (b, c, P_X // BR)),
                  pl.BlockSpec((1, q, B_BC), lambda b, c: (b, c, P_BM // B_BC)),
                  pl.BlockSpec((1, q, B_BC), lambda b, c: (b, c, P_CM // B_BC)),
                  pl.BlockSpec((1, q, 128), lambda b, c: (b, c, 0)),
                  pl.BlockSpec((1, 1, B_CONV - 1, B_CONV_DIM), lambda b, c: (layer, b, 0, 0)),
                  pl.BlockSpec((1, 1, NHEAD, HEAD, B_STATE), lambda b, c: (layer, b, 0, 0, 0))]
        + [_const_spec(a, 2) for a in consts],
        out_specs=[pl.BlockSpec((1, q, BR), lambda b, c: (b, c, 0)),
                   pl.BlockSpec((1, B_CONV - 1, B_CONV_DIM), lambda b, c: (b, 0, 0)),
                   pl.BlockSpec((1, NHEAD, HEAD, B_STATE), lambda b, c: (b, 0, 0, 0))],
        out_shape=[jax.ShapeDtypeStruct((nb, l, BR), BF16),
                   jax.ShapeDtypeStruct((nb, B_CONV - 1, B_CONV_DIM), F32),
                   jax.ShapeDtypeStruct((nb, NHEAD, HEAD, B_STATE), F32)],
        scratch_shapes=[pltpu.VMEM((qp + 8, B_CONV_DIM), F32), pltpu.VMEM((NHEAD, HEAD, B_STATE), F32),
                        pltpu.VMEM((qp, BR + 128), F32)],
        compiler_params=_cparams("parallel", "arbitrary"),
    )(proj3, proj3, proj3, proj3, dt3, conv0, ssm0, *consts)


def _gelu_tanh(y):
    return 0.5 * y * (1.0 + jnp.tanh(0.7978845608028654 * (y + 0.044715 * (y * y * y))))


def _c_window(qq):
    first = qq * C_LN // C_STATE * C_GROUP
    return min(first // 128 * 128, BR - C_WIN)


def _c_prompt_kernel(u_ref, h0_ref, lb_ref, bq_ref, cq_ref, dsk_ref, wg_ref, bg_ref, o_ref, hf_ref,
                     bu_scr, hs_scr, h_scr, *, t):
    @pl.when(pl.program_id(0) == 0)
    def _():
        h_scr[...] = h0_ref[...]

    u = u_ref[...]
    ub = u.astype(BF16)
    ntile = 2 * C_LN // 128
    for qq in range(C_Q):
        ws = _c_window(qq)
        bu = _mm(ub[:, ws:ws + C_WIN], bq_ref[qq])
        for j in range(ntile):
            bu_scr[j, pl.ds(qq, t, stride=C_Q), :] = bu[:, j * 128:(j + 1) * 128]
    lb = [lb_ref[:, j * 128:(j + 1) * 128] for j in range(ntile)]
    half = ntile // 2

    def step(i, h):
        r0 = pl.multiple_of(i * C_Q, C_Q)
        new = []
        for j in range(half):
            new.append(lb[j] * h[j] - lb[half + j] * h[half + j] + bu_scr[j, pl.ds(r0, C_Q), :])
        for j in range(half):
            new.append(lb[j] * h[half + j] + lb[half + j] * h[j] + bu_scr[half + j, pl.ds(r0, C_Q), :])
        for j in range(ntile):
            hs_scr[j, pl.ds(r0, C_Q), :] = new[j]
        return tuple(new)

    h = lax.fori_loop(0, t, step, tuple(h_scr[:, j * 128:(j + 1) * 128] for j in range(ntile)), unroll=8)
    for j in range(ntile):
        h_scr[:, j * 128:(j + 1) * 128] = h[j]
    hf_ref[...] = h_scr[...]
    ytile = [None] * (BR // 128)
    for qq in range(C_Q):
        ws = _c_window(qq) // 128
        hq = jnp.concatenate([hs_scr[j, pl.ds(qq, t, stride=C_Q), :] for j in range(ntile)], axis=-1)
        yq = _mm(hq.astype(BF16), cq_ref[qq])
        for j in range(C_WIN // 128):
            part = yq[:, j * 128:(j + 1) * 128]
            ytile[ws + j] = part if ytile[ws + j] is None else ytile[ws + j] + part
    y = _gelu_tanh(jnp.concatenate(ytile, axis=-1) + dsk_ref[...] * u)
    o_ref[...] = (y * _sigmoid(_bmm(y, wg_ref[...]) + bg_ref[...])).astype(o_ref.dtype)


def _mixer_c_prompt(proj, h0, prm, *, t):
    l = proj.shape[0]
    consts = [prm[k] for k in ('c_lb8', 'c_bq', 'c_cq', 'c_dsk', 'c_wg', 'c_bg')]
    return pl.pallas_call(
        functools.partial(_c_prompt_kernel, t=t), grid=(l // t,),
        in_specs=[pl.BlockSpec((t, BR), lambda c: (c, P_U // BR)), _const_spec(h0, 1)]
        + [_const_spec(a, 1) for a in consts],
        out_specs=[pl.BlockSpec((t, BR), lambda c: (c, 0)), pl.BlockSpec((C_Q, 2 * C_LN), lambda c: (0, 0))],
        out_shape=[jax.ShapeDtypeStruct((l, BR), BF16), jax.ShapeDtypeStruct((C_Q, 2 * C_LN), F32)],
        scratch_shapes=[pltpu.VMEM((2 * C_LN // 128, C_Q * t, 128), F32),
                        pltpu.VMEM((2 * C_LN // 128, C_Q * t, 128), F32),
                        pltpu.VMEM((C_Q, 2 * C_LN), F32)],
        compiler_params=_cparams("arbitrary"),
    )(proj, h0, *consts)


def _c_sample_kernel(u_ref, hr0_ref, hi0_ref, lbr_ref, lbi_ref, bf_ref, cf_ref, dsk_ref, wg_ref, bg_ref,
                     o_ref, hr_ref, hi_ref, *, nt):
    hr = hr0_ref[...]
    hi = hi0_ref[...]
    lbr = lbr_ref[...]
    lbi = lbi_ref[...]
    for t in range(nt):
        u = u_ref[t]
        bu = _bmm(u, bf_ref[...])
        hr, hi = lbr * hr - lbi * hi + bu[:, :C_NS], lbr * hi + lbi * hr + bu[:, C_NS:]
        y = _bmm(hr, cf_ref[:C_NS, :]) + _bmm(hi, cf_ref[C_NS:, :]) + dsk_ref[...] * u
        y = _gelu_tanh(y)
        o_ref[t] = (y * _sigmoid(_bmm(y, wg_ref[...]) + bg_ref[...])).astype(o_ref.dtype)
    hr_ref[...] = hr
    hi_ref[...] = hi


def _mixer_c_sample(u_t, hr0, hi0, prm):
    nt, nb, _ = u_t.shape
    args = [u_t, hr0, hi0] + [prm[k] for k in ('c_lbr', 'c_lbi', 'c_bf', 'c_cf', 'c_dsk', 'c_wg', 'c_bg')]
    return pl.pallas_call(
        functools.partial(_c_sample_kernel, nt=nt), grid=(1,),
        in_specs=[_const_spec(a, 1) for a in args],
        out_specs=[pl.BlockSpec((nt, nb, BR), lambda c: (0, 0, 0)), pl.BlockSpec((nb, C_NS), lambda c: (0, 0)),
                   pl.BlockSpec((nb, C_NS), lambda c: (0, 0))],
        out_shape=[jax.ShapeDtypeStruct((nt, nb, BR), BF16), jax.ShapeDtypeStruct((nb, C_NS), F32),
                   jax.ShapeDtypeStruct((nb, C_NS), F32)],
        compiler_params=_cparams("arbitrary"),
    )(*args)


def _d_kernel(r_ref, k_ref, v_ref, l_ref, sh0_ref, wkv0_ref, mu_ref, w0_ref, a0_ref, kk_ref, ka_ref, rk_ref,
              gnw_ref, gnb_ref, wl_ref, al_ref, gl_ref, blk_ref, y_ref, sh_ref, wkv_ref, x_scr, ht_scr, *, t, tp,
              pa, pi, ps):
    c = pl.program_id(1)

    @pl.when(c == 0)
    def _():
        x_scr[...] = jnp.zeros_like(x_scr)
        x_scr[7:8, :] = sh0_ref[0, 0]
        zero = jnp.zeros((HEAD, HEAD), F32)
        for p in range(NHEAD // 2):
            ht_scr[p, 0:HEAD, :] = jnp.concatenate([wkv0_ref[0, 0, 2 * p], zero], axis=-1)
            ht_scr[p, HEAD:, :] = jnp.concatenate([zero, wkv0_ref[0, 0, 2 * p + 1]], axis=-1)

    x_scr[8:8 + t, 0:BR] = r_ref[0]
    x_scr[8:8 + t, BR:2 * BR] = k_ref[0]
    x_scr[8:8 + t, 2 * BR:3 * BR] = v_ref[0]
    x_scr[8:8 + t, 3 * BR:] = l_ref[0]
    cur = x_scr[8:8 + tp, :]
    xm = cur + (x_scr[7:7 + tp, :] - cur) * mu_ref[...]
    last = x_scr[7 + t:8 + t, :]
    x_scr[7:8, :] = last
    sh_ref[0] = last
    r = xm[:, :BR]
    k = xm[:, BR:2 * BR]
    v = xm[:, 2 * BR:3 * BR]
    xl = xm[:, 3 * BR:]
    lw = -jnp.exp(-_softplus(-(w0_ref[...] + _bmm(jnp.tanh(xl), wl_ref[...]))) - 0.5)
    a = _sigmoid(a0_ref[...] + _bmm(xl, al_ref[...]))
    g = _bmm(_sigmoid(xl), gl_ref[...])
    blk = blk_ref[...]
    kkr = k * kk_ref[...]
    kk = kkr / jnp.maximum(jnp.sqrt(_head_sums(kkr * kkr, blk, 3)), 1e-12)
    k2 = k * (1.0 + (a - 1.0) * ka_ref[...])
    if t != tp:
        live = (_iota((tp, 1), 0) < t).astype(F32)
        r, k2, v, kk, lw = r * live, k2 * live, v * live, kk * live, lw * live
    bv = kk * a
    row = _iota((tp, tp), 0)
    col = _iota((tp, tp), 1)
    cum = _pdot((row >= col).astype(F32), lw, 'nn', 6)
    cend = cum[tp - 1:tp, :]
    cmid = cum[tp // 2 - 1:tp // 2, :]
    e_in = jnp.exp(cum - cmid)
    e_neg = jnp.exp(cmid - cum)
    e_end = jnp.exp(cend - cum)
    rd = r * jnp.exp(cum)
    kkd = kk * jnp.exp(cum - lw)
    rd_c = r * e_in
    kkd_c = kk * jnp.exp(cum - lw - cmid)
    ks = k2 * e_neg
    bs = bv * e_neg
    kse = k2 * e_end
    bse = bv * e_end
    dend = jnp.exp(cend)
    tp2 = 2 * tp
    row2 = _iota((tp2, tp2), 0) % tp
    col2 = _iota((tp2, tp2), 1) % tp
    incl = row2 >= col2
    strict = row2 > col2
    eye = (_iota((tp2, tp2), 0) == _iota((tp2, tp2), 1)).astype(F32)
    left = _iota((tp, 2 * HEAD), 1) < HEAD
    base = min(16, tp)
    blocks = []
    sz = base
    while sz <= tp:
        blocks.append((row2 // sz) == (col2 // sz))
        sz *= 2

    def stack(x, p):
        xp = x[:, 2 * HEAD * p:2 * HEAD * (p + 1)]
        return jnp.concatenate([jnp.where(left, xp, 0.0), jnp.where(left, 0.0, xp)], axis=0)

    ys = []
    for p in range(NHEAD // 2):
        kkd_s, bs_s, ks_s, rd_s, v_s = stack(kkd, p), stack(bs, p), stack(ks, p), stack(rd, p), stack(v, p)
        kkd_cs, rd_cs = stack(kkd_c, p), stack(rd_c, p)
        akb = jnp.where(strict, _pdot(kkd_cs, bs_s, 'nt', pa), 0.0)
        akk = jnp.where(strict, _pdot(kkd_cs, ks_s, 'nt', pa), 0.0)
        arb = jnp.where(incl, _pdot(rd_cs, bs_s, 'nt', pa), 0.0)
        ark = jnp.where(incl, _pdot(rd_cs, ks_s, 'nt', pa), 0.0)
        nb_ = jnp.where(blocks[0], akb, 0.0)
        inv = eye - nb_
        pw = _pdot(nb_, nb_, 'nn', pi)
        n = 2
        while n < base:
            inv = inv + _pdot(inv, pw, 'nn', pi)
            n *= 2
            if n < base:
                pw = _pdot(pw, pw, 'nn', pi)
        for lvl in range(1, len(blocks)):
            off = jnp.where(blocks[lvl] & ~blocks[lvl - 1], akb, 0.0)
            inv = inv - _pdot(inv, _pdot(off, inv, 'nn', pi), 'nn', pi)
        w1 = _pdot(inv, kkd_s, 'nn', pi)
        w2 = _pdot(inv, _pdot(akk, v_s, 'nn', pa), 'nn', pi)
        ht = ht_scr[p]
        u = _pdot(w1, ht, 'nt', ps) + w2
        y_s = _pdot(rd_s, ht, 'nt', ps) + _pdot(ark, v_s, 'nn', pa) - _pdot(arb, u, 'nn', pa)
        ys.append(y_s[:tp] + y_s[tp:])
        ht_scr[p] = (ht * dend[:, 2 * HEAD * p:2 * HEAD * (p + 1)] + _pdot(v_s, stack(kse, p), 'tn', ps)
                     - _pdot(u, stack(bse, p), 'tn', ps))
    y = jnp.concatenate(ys, axis=-1)
    mean = _head_sums(y, blk, 3) * (1.0 / HEAD)
    d = y - mean
    var = _head_sums(d * d, blk, 3) * (1.0 / HEAD)
    yn = d * lax.rsqrt(var + RWKV_GN_EPS) * gnw_ref[...] + gnb_ref[...]
    bonus = _head_sums(r * k2 * rk_ref[...], blk, 3)
    y_ref[0] = (((yn + bonus * v) * g)[:t]).astype(y_ref.dtype)

    @pl.when(c == pl.num_programs(1) - 1)
    def _():
        for p in range(NHEAD // 2):
            wkv_ref[0, 2 * p] = ht_scr[p, 0:HEAD, 0:HEAD]
            wkv_ref[0, 2 * p + 1] = ht_scr[p, HEAD:, HEAD:]


def _mixer_d(proj3, shift0, wkv0, layer, prm, *, t):
    nb, l, _ = proj3.shape
    tp = max(t, 8)
    consts = [prm[k] for k in ('d_mu', 'd_w0', 'd_a0', 'd_kk', 'd_ka', 'd_rk', 'd_gnw', 'd_gnb',
                               'd_wl', 'd_al', 'd_gl', 'ds_blk')]
    return pl.pallas_call(
        functools.partial(_d_kernel, t=t, tp=tp, pa=D_PASSES[0], pi=D_PASSES[1], ps=D_PASSES[2]),
        grid=(nb, l // t),
        in_specs=[pl.BlockSpec((1, t, BR), lambda b, c: (b, c, P_R // BR)),
                  pl.BlockSpec((1, t, BR), lambda b, c: (b, c, P_DK // BR)),
                  pl.BlockSpec((1, t, BR), lambda b, c: (b, c, P_DV // BR)),
                  pl.BlockSpec((1, t, D_LORA), lambda b, c: (b, c, P_DL // D_LORA)),
                  pl.BlockSpec((1, 1, 1, D_COLS), lambda b, c: (layer, b, 0, 0)),
                  pl.BlockSpec((1, 1, NHEAD, HEAD, HEAD), lambda b, c: (layer, b, 0, 0, 0))]
        + [_const_spec(a, 2) for a in consts],
        out_specs=[pl.BlockSpec((1, t, BR), lambda b, c: (b, c, 0)),
                   pl.BlockSpec((1, 1, D_COLS), lambda b, c: (b, 0, 0)),
                   pl.BlockSpec((1, NHEAD, HEAD, HEAD), lambda b, c: (b, 0, 0, 0))],
        out_shape=[jax.ShapeDtypeStruct((nb, l, BR), BF16),
                   jax.ShapeDtypeStruct((nb, 1, D_COLS), F32),
                   jax.ShapeDtypeStruct((nb, NHEAD, HEAD, HEAD), F32)],
        scratch_shapes=[pltpu.VMEM((tp + 8, D_COLS), F32), pltpu.VMEM((NHEAD // 2, 2 * HEAD, 2 * HEAD), F32)],
        compiler_params=_cparams("parallel", "arbitrary"),
    )(proj3, proj3, proj3, proj3, shift0, wkv0, *consts)


def _ds_prep_kernel(xr_ref, xk_ref, xv_ref, xg_ref, xwa_ref, sr_ref, sk_ref, sv_ref, sg_ref, swa_ref,
                    mr_ref, mk_ref, mv_ref, mg_ref, mwa_ref, w0_ref, a0_ref, kk_ref, ka_ref,
                    wl_ref, al_ref, gl_ref, blk_ref, *outs, nt):
    r_o, w_o, k_o, v_o, kk_o, bv_o, g_o = outs
    nb = sr_ref.shape[1]

    def mixed(x_ref, s_ref, m_ref, t):
        cur = x_ref[pl.ds(t, nb, stride=nt), :]
        prev = s_ref[0] if t == 0 else x_ref[pl.ds(t - 1, nb, stride=nt), :]
        return cur + (prev - cur) * m_ref[...]

    for t in range(nt):
        r = mixed(xr_ref, sr_ref, mr_ref, t)
        k = mixed(xk_ref, sk_ref, mk_ref, t)
        v = mixed(xv_ref, sv_ref, mv_ref, t)
        xg = mixed(xg_ref, sg_ref, mg_ref, t)
        xwa = mixed(xwa_ref, swa_ref, mwa_ref, t)
        lw = -jnp.exp(-_softplus(-(w0_ref[...] + _bmm(jnp.tanh(xwa), wl_ref[...]))) - 0.5)
        a = _sigmoid(a0_ref[...] + _bmm(xwa, al_ref[...]))
        kkr = k * kk_ref[...]
        kk = kkr / jnp.maximum(jnp.sqrt(_sel_mm(kkr * kkr, blk_ref[...], 3)), 1e-12)
        r_o[t] = r.T
        w_o[t] = jnp.exp(lw).T
        k_o[t] = (k * (1.0 + (a - 1.0) * ka_ref[...])).T
        v_o[t] = v.T
        kk_o[t] = kk.T
        bv_o[t] = (kk * a).T
        g_o[t] = _bmm(_sigmoid(xg), gl_ref[...]).T


def _ds_rec_kernel(s_ref, r_ref, w_ref, k_ref, v_ref, kk_ref, bv_ref, g_ref, rk_ref, gnw_ref, gnb_ref,
                   o_ref, sn_ref, y_scr, *, nt):
    for hh in range(2):
        c0 = hh * HEAD

        def row(i, carry):
            s = s_ref[0, hh, i]
            for t in range(nt):
                sa = jnp.sum(s * kk_ref[t, c0:c0 + HEAD, :], axis=0, keepdims=True)
                s = (s * w_ref[t, c0:c0 + HEAD, :] - sa * bv_ref[t, c0:c0 + HEAD, :]
                     + v_ref[t, pl.ds(c0 + i, 1), :] * k_ref[t, c0:c0 + HEAD, :])
                y_scr[t, pl.ds(c0 + i, 1), :] = jnp.sum(s * r_ref[t, c0:c0 + HEAD, :], axis=0, keepdims=True)
            sn_ref[hh, i] = s
            return carry

        lax.fori_loop(0, HEAD, row, 0)
    for t in range(nt):
        halves = []
        for hh in range(2):
            sl = slice(hh * HEAD, (hh + 1) * HEAD)
            y = y_scr[t, sl, :]
            mean = jnp.mean(y, axis=0, keepdims=True)
            d = y - mean
            var = jnp.mean(d * d, axis=0, keepdims=True)
            yn = d * lax.rsqrt(var + RWKV_GN_EPS) * gnw_ref[sl, :] + gnb_ref[sl, :]
            bonus = jnp.sum(r_ref[t, sl, :] * k_ref[t, sl, :] * rk_ref[sl, :], axis=0, keepdims=True)
            halves.append((yn + bonus * v_ref[t, sl, :]) * g_ref[t, sl, :])
        o_ref[t] = jnp.concatenate(halves, axis=0).T.astype(o_ref.dtype)


def _mixer_d_sample(proj, shift0, wkv_t, layer, prm, *, nt):
    m = proj.shape[0]
    nb = m // nt
    npair = NHEAD // 2
    w2 = 2 * HEAD

    def xcol(off):
        return pl.BlockSpec((m, w2), lambda p: (0, off // w2 + p))

    def scol(off):
        return pl.BlockSpec((1, nb, w2), lambda p: (layer, 0, off // w2 + p))

    def vcol(off):
        return pl.BlockSpec((1, w2), lambda p: (0, off // w2 + p))

    x_specs = [xcol(P_R), xcol(P_DK), xcol(P_DV),
               pl.BlockSpec((m, w2), lambda p: (0, P_DL // w2)), pl.BlockSpec((m, w2), lambda p: (0, P_DL // w2 + 1))]
    s_specs = [scol(0), scol(BR), scol(2 * BR),
               pl.BlockSpec((1, nb, w2), lambda p: (layer, 0, 3 * BR // w2)),
               pl.BlockSpec((1, nb, w2), lambda p: (layer, 0, 3 * BR // w2 + 1))]
    m_specs = [vcol(0), vcol(BR), vcol(2 * BR),
               pl.BlockSpec((1, w2), lambda p: (0, 3 * BR // w2)), pl.BlockSpec((1, w2), lambda p: (0, 3 * BR // w2 + 1))]
    p_specs = [vcol(0)] * 4
    w_specs = [pl.BlockSpec((w2, w2), lambda p: (0, p))] * 3
    tile = jax.ShapeDtypeStruct((nt, BR, nb), F32)
    tile_spec = pl.BlockSpec((nt, w2, nb), lambda p: (0, p, 0))
    tiles = pl.pallas_call(
        functools.partial(_ds_prep_kernel, nt=nt), grid=(npair,),
        in_specs=x_specs + s_specs + m_specs + p_specs + w_specs + [_const_spec(prm['ds_blk'], 1)],
        out_specs=[tile_spec] * 7, out_shape=[tile] * 7,
        compiler_params=_cparams("parallel"),
    )(*([proj] * 5), *([shift0] * 5), *([prm['d_mu']] * 5), prm['d_w0'], prm['d_a0'], prm['d_kk'], prm['d_ka'],
      prm['ds_wl'], prm['ds_al'], prm['ds_gl'], prm['ds_blk'])
    col_spec = pl.BlockSpec((w2, nb), lambda p: (p, 0))
    return pl.pallas_call(
        functools.partial(_ds_rec_kernel, nt=nt), grid=(npair,),
        in_specs=[pl.BlockSpec((1, 2, HEAD, HEAD, nb), lambda p: (layer, p, 0, 0, 0))] + [tile_spec] * 7
        + [col_spec] * 3,
        out_specs=[pl.BlockSpec((nt, nb, w2), lambda p: (0, 0, p)),
                   pl.BlockSpec((2, HEAD, HEAD, nb), lambda p: (p, 0, 0, 0))],
        out_shape=[jax.ShapeDtypeStruct((nt, nb, BR), BF16), jax.ShapeDtypeStruct((NHEAD, HEAD, HEAD, nb), F32)],
        scratch_shapes=[pltpu.VMEM((nt, w2, nb), F32)],
        compiler_params=_cparams("parallel"),
    )(wkv_t, *tiles, prm['ds_rk'], prm['ds_gnw'], prm['ds_gnb'])


_IN_SPLITS = (BR, BR, BR, BR, BR, B_BC, B_BC, NHEAD, BR, BR, D_LW, BR, BR, D_LA, D_LG, N_BRANCH * D_MODEL)
_IN_NAMES = ('q', 'k', 'v', 'z', 'x', 'bm', 'cm', 'dt', 'u', 'r', 'wl', 'dk', 'dv', 'al', 'gl', 'gate')
_PACK_ORDER = ('q', 'k', 'v', 'z', 'u', 'x', 'r', 'dk', 'dv', 'gl', 'wl', 'al', 'bm', 'cm', 'gate')
_D_SPLITS = (BR, D_LW, BR, BR, D_LA, D_LG)
_D_PERM = np.concatenate([np.arange(o, o + n) for o, n in (
    (0, BR), (BR + D_LW, BR), (2 * BR + D_LW, BR), (3 * BR + D_LW + D_LA, D_LG), (BR, D_LW), (3 * BR + D_LW, D_LA))])
_D_INV = np.argsort(_D_PERM)


def _head_block(width, group):
    idx = np.arange(width) // group
    return jnp.asarray((idx[:, None] == idx[None, :]).astype(np.float32))


def _layer_params(i, P, nb_sample):
    prm = {}
    offs = np.cumsum((0,) + _IN_SPLITS)
    wt = jnp.transpose(P['w_in'], (2, 0, 1))[:, i, :]
    cols = {n: wt[offs[j]:offs[j + 1]] for j, n in enumerate(_IN_NAMES)}
    prm['w_in'] = jnp.concatenate([cols[n] for n in _PACK_ORDER], axis=0).astype(BF16)
    prm['w_dt'] = jnp.pad(cols['dt'], ((0, 128 - NHEAD), (0, 0))).astype(BF16)
    prm['norm1'] = P['norm1'][i][None]
    prm['norm2'] = P['norm2'][i][None]
    prm['w_branch'] = P['w_branch'][i].astype(BF16)
    prm['w_out'] = P['w_out'][i].astype(BF16)
    prm['w_ff1'] = P['w_ff1'][i].astype(BF16)
    prm['w_ff2'] = P['w_ff2'][i].astype(BF16)
    prm['a_qg'] = jnp.tile(P['a_q_gain'][i], NHEAD)[None]
    prm['a_kg'] = jnp.tile(P['a_k_gain'][i], NHEAD)[None]
    prm['blk64'] = _head_block(BR, HEAD)
    pad12 = lambda v: jnp.pad(v, (0, 128 - NHEAD))[None]
    prm['b_cw'] = P['b_conv_w'][i]
    prm['b_cb'] = P['b_conv_b'][i][None]
    prm['b_dtb'] = pad12(P['b_dt_bias'][i])
    prm['b_a'] = pad12(-jnp.exp(P['b_a_log'][i]))
    prm['b_dsk'] = pad12(P['b_d'][i])
    prm['b_nw'] = P['b_norm'][i][None]
    prm['b_sel'] = jnp.asarray(np.eye(16, 128, dtype=np.float32))
    prm['b_gblk'] = _head_block(BR, BR // B_GROUPS) * (B_GROUPS / BR)
    a_re, a_im = P['c_a_re'][i], P['c_a_im'][i]
    step = jnp.exp(P['c_log_step'][i])[:, None]
    mag = jnp.exp(a_re * step)
    lb_re, lb_im = mag * jnp.cos(a_im * step), mag * jnp.sin(a_im * step)
    den = a_re * a_re + a_im * a_im
    f_re = ((lb_re - 1.0) * a_re + lb_im * a_im) / den
    f_im = (lb_im * a_re - (lb_re - 1.0) * a_im) / den
    b_re, b_im = P['c_b_re'][i], P['c_b_im'][i]
    bb_re = f_re[..., None] * b_re - f_im[..., None] * b_im
    bb_im = f_re[..., None] * b_im + f_im[..., None] * b_re
    eye_g = jnp.eye(C_NG, dtype=F32)
    bfull_re = jnp.einsum('gpc,gh->gchp', bb_re, eye_g).reshape(BR, C_NS)
    bfull_im = jnp.einsum('gpc,gh->gchp', bb_im, eye_g).reshape(BR, C_NS)
    cfull_re = jnp.einsum('gcp,gh->gphc', P['c_c_re'][i], eye_g).reshape(C_NS, BR)
    cfull_im = jnp.einsum('gcp,gh->gphc', P['c_c_im'][i], eye_g).reshape(C_NS, BR)
    prm['c_lbr'] = lb_re.reshape(1, C_NS)
    prm['c_lbi'] = lb_im.reshape(1, C_NS)
    prm['c_lb8'] = jnp.concatenate([lb_re.reshape(C_Q, C_LN), lb_im.reshape(C_Q, C_LN)], axis=1)
    prm['c_bf'] = jnp.concatenate([bfull_re, bfull_im], axis=1).astype(BF16)
    prm['c_cf'] = jnp.concatenate([cfull_re, -cfull_im], axis=0).astype(BF16)
    bq, cq = [], []
    for qq in range(C_Q):
        ws, s0 = _c_window(qq), qq * C_LN
        bq.append(jnp.concatenate([bfull_re[ws:ws + C_WIN, s0:s0 + C_LN], bfull_im[ws:ws + C_WIN, s0:s0 + C_LN]], 1))
        cq.append(jnp.concatenate([cfull_re[s0:s0 + C_LN, ws:ws + C_WIN], -cfull_im[s0:s0 + C_LN, ws:ws + C_WIN]], 0))
    prm['c_bq'] = jnp.stack(bq).astype(BF16)
    prm['c_cq'] = jnp.stack(cq).astype(BF16)
    prm['c_dsk'] = P['c_d'][i][None]
    prm['c_wg'] = P['c_w_glu'][i].astype(BF16)
    prm['c_bg'] = P['c_b_glu'][i][None]
    prm['d_mu'] = P['d_mu'][i][_D_PERM][None]
    for k_, n_ in (('d_w0', 'd_w0'), ('d_a0', 'd_a0'), ('d_kk', 'd_k_k'), ('d_ka', 'd_k_a'),
                   ('d_gnw', 'd_gn_w'), ('d_gnb', 'd_gn_b')):
        prm[k_] = P[n_][i][None]
    prm['d_rk'] = P['d_r_k'][i].reshape(1, BR)
    zl = lambda r: jnp.zeros((r, BR), F32)
    prm['d_gl'] = jnp.concatenate([P['d_g_lora'][i], zl(D_LW + D_LA)], 0).astype(BF16)
    prm['d_wl'] = jnp.concatenate([zl(D_LG), P['d_w_lora'][i], zl(D_LA)], 0).astype(BF16)
    prm['d_al'] = jnp.concatenate([zl(D_LG + D_LW), P['d_a_lora'][i]], 0).astype(BF16)
    prm['d_blk'] = prm['blk64'].astype(BF16)
    prm['ds_wl'] = jnp.concatenate([P['d_w_lora'][i], zl(D_LA)], 0).astype(BF16)
    prm['ds_al'] = jnp.concatenate([zl(D_LW), P['d_a_lora'][i]], 0).astype(BF16)
    prm['ds_gl'] = P['d_g_lora'][i].astype(BF16)
    prm['ds_blk'] = _head_block(2 * HEAD, HEAD).astype(BF16)
    for k_, n_ in (('ds_rk', 'd_r_k'), ('ds_gnw', 'd_gn_w'), ('ds_gnb', 'd_gn_b')):
        prm[k_] = jnp.broadcast_to(P[n_][i].reshape(BR, 1), (BR, nb_sample))
    return prm


def _layer(x, states, layer, prm, tabs, *, nb, prompt):
    m = x.shape[0]
    l = m // nb
    tm = min(m, 512)
    proj, dt = _norm_mm(x, prm['norm1'], prm['w_in'], prm['w_dt'], tm=tm, tn=1024)
    proj3 = proj.reshape(nb, l, P_COLS)
    dt3 = dt.reshape(nb, l, 128)
    qn, kv = _a_prep(proj, prm['a_qg'], prm['a_kg'], prm['d_blk'], tm=tm)
    if prompt:
        oa = _a_attn_prompt(qn, kv, tabs['a_bias'])
        kv_new = kv[m - min(A_WIN, m):].reshape(nb, min(A_WIN, m), 2, NHEAD, HEAD)
    else:
        oa = _a_attn_sample(qn, kv, states['kv'], layer, tabs['a_sample']).reshape(m, BR)
        kv_new = kv.reshape(nb, l, 2, NHEAD, HEAD)
    ob, conv_new, ssm_new = _mixer_b(proj3, dt3, states['conv'], states['ssm'], layer, prm, q=min(l, 128))
    if prompt:
        oc, hf = _mixer_c_prompt(proj, states['s5'], prm, t=min(l, 256))
        s5_new = jnp.stack([hf[:, :C_LN].reshape(C_NG, C_STATE), hf[:, C_LN:].reshape(C_NG, C_STATE)], -1)[None]
    else:
        u_t = proj3[:, :, P_U:P_U + BR].transpose(1, 0, 2)
        s0 = states['s5'][layer]
        oc_t, hr, hi = _mixer_c_sample(u_t, s0[..., 0].reshape(nb, C_NS), s0[..., 1].reshape(nb, C_NS), prm)
        oc = oc_t.transpose(1, 0, 2).reshape(m, BR)
        s5_new = jnp.stack([hr.reshape(nb, C_NG, C_STATE), hi.reshape(nb, C_NG, C_STATE)], -1)
    if prompt:
        od, sh_new, wkv_new = _mixer_d(proj3, states['shift'], states['wkv'], layer, prm, t=min(l, D_CHUNK))
        od = od.reshape(m, BR)
        shift_new = sh_new.reshape(nb, D_COLS)[:, _D_INV]
    else:
        od_t, wkv_t = _mixer_d_sample(proj, states['shift'], states['wkv_t'], layer, prm, nt=l)
        od = od_t.transpose(1, 0, 2).reshape(m, BR)
        shift_new = proj3[:, l - 1, P_R:P_R + D_COLS][:, _D_INV]
        wkv_new = wkv_t.transpose(3, 0, 1, 2)
    merged = _merge([oa, ob.reshape(m, BR), oc, od.reshape(m, BR)], prm['w_branch'], proj, tm=tm, tn=512)
    x = _mm_res(merged, prm['w_out'], x, tm=tm, tn=1024, tk=D_MODEL)
    hid = _norm_mm(x, prm['norm2'], prm['w_ff1'], tm=tm, tn=1024, relu2=True, out_dtype=BF16)
    x = _mm_res(hid, prm['w_ff2'], x, tm=tm, tn=1024, tk=2048)
    return x, (kv_new, conv_new, ssm_new, s5_new, shift_new, wkv_new)


def kernel(x_prompt, x_sample, cache_kv_a, state_conv, state_ssm, state_s5, state_shift, state_wkv,
           norm1, w_in, a_q_gain, a_k_gain,
           b_conv_w, b_conv_b, b_dt_bias, b_a_log, b_d, b_norm,
           c_a_re, c_a_im, c_log_step, c_b_re, c_b_im, c_c_re, c_c_im, c_d, c_w_glu, c_b_glu,
           d_mu, d_w0, d_w_lora, d_a0, d_a_lora, d_g_lora, d_k_k, d_k_a, d_r_k, d_gn_w, d_gn_b,
           w_branch, w_out, norm2, w_ff1, w_ff2):
    P = dict(norm1=norm1, w_in=w_in, a_q_gain=a_q_gain, a_k_gain=a_k_gain,
             b_conv_w=b_conv_w, b_conv_b=b_conv_b, b_dt_bias=b_dt_bias, b_a_log=b_a_log,
             b_d=b_d, b_norm=b_norm,
             c_a_re=c_a_re, c_a_im=c_a_im, c_log_step=c_log_step, c_b_re=c_b_re, c_b_im=c_b_im,
             c_c_re=c_c_re, c_c_im=c_c_im, c_d=c_d, c_w_glu=c_w_glu, c_b_glu=c_b_glu,
             d_mu=d_mu, d_w0=d_w0, d_w_lora=d_w_lora, d_a0=d_a0, d_a_lora=d_a_lora,
             d_g_lora=d_g_lora, d_k_k=d_k_k, d_k_a=d_k_a, d_r_k=d_r_k, d_gn_w=d_gn_w, d_gn_b=d_gn_b,
             w_branch=w_branch, w_out=w_out, norm2=norm2, w_ff1=w_ff1, w_ff2=w_ff2)
    depth = w_in.shape[0]
    bp, lp, _ = x_prompt.shape
    bs, ls, _ = x_sample.shape
    assert bp == 1 and ls == A_T
    tabs = {'a_bias': _a_band_bias(), 'a_sample': _a_sample_tables(cache_kv_a.shape[2])}
    zero_states = {'conv': jnp.zeros((1, bp, B_CONV - 1, B_CONV_DIM), F32),
                   'ssm': jnp.zeros((1, bp, NHEAD, HEAD, B_STATE), F32),
                   's5': jnp.zeros((C_Q, 2 * C_LN), F32),
                   'shift': jnp.zeros((1, bp, 1, D_COLS), F32),
                   'wkv': jnp.zeros((1, bp, NHEAD, HEAD, HEAD), F32)}
    samp_states = {'kv': jnp.transpose(cache_kv_a, (0, 1, 3, 4, 5, 2)), 'conv': state_conv, 'ssm': state_ssm, 's5': state_s5,
                   'shift': state_shift[:, :, _D_PERM], 'wkv_t': jnp.transpose(state_wkv, (0, 2, 3, 4, 1))}
    yp = x_prompt.reshape(bp * lp, D_MODEL)
    ys = x_sample.reshape(bs * ls, D_MODEL)
    new_p, new_s = [], []
    for i in range(depth):
        prm = _layer_params(i, P, bs)
        yp, st = _layer(yp, zero_states, 0, prm, tabs, nb=bp, prompt=True)
        new_p.append(st)
        ys, st = _layer(ys, samp_states, i, prm, tabs, nb=bs, prompt=False)
        new_s.append(st)
    outs_p = [jnp.stack(z) for z in zip(*new_p)]
    outs_s = [jnp.stack(z) for z in zip(*new_s)]
    res = [yp.reshape(bp, lp, D_MODEL), ys.reshape(bs, ls, D_MODEL)]
    for a, b in zip(outs_p, outs_s):
        res += [a, b]
    return tuple(res)
```
